```python
import math
import jax, jax.numpy as jnp
from jax import lax
import numpy as np

D_MODEL = 1024
BATCH = 8
SEQ = 4096
DEPTH = 4

GRID_W = 64
CTX_LEN = 256
N_EVEN = (DEPTH + 1) // 2
N_ODD = DEPTH // 2
S5_WIDTH = D_MODEL // 2
S5_GROUP = 16
S5_GROUPS = S5_WIDTH // S5_GROUP
S5_STATE = 64
NA_WIDTH = D_MODEL - S5_WIDTH
NA_HEAD_DIM = 64
NA_HEADS = NA_WIDTH // NA_HEAD_DIM
NA_WIN_R = 8
NA_WIN_C = 16
IN_WIDTH = S5_WIDTH + 3 * NA_WIDTH
MIX_WIDTH = S5_WIDTH + NA_WIDTH
D_FF = 4 * D_MODEL
N_MOD = 6
EPS = 1e-6
NEG_INF = -1e30

kernel_name = 'hybrid_s5_natten_fnet_dit_trunk'


def rms_norm(x, g):
    xf = x.astype(jnp.float32)
    y = xf * lax.rsqrt(jnp.mean(jnp.square(xf), axis=-1, keepdims=True) + EPS)
    return (y * g.astype(jnp.float32)).astype(x.dtype)


def modulate(x, shift, scale):
    return x * (1.0 + scale[:, None, :]) + shift[:, None, :]


def s5_discretise(lam_re, lam_im, log_dt, b_re, b_im):
    f32 = jnp.float32
    lam_re = jnp.minimum(lam_re.astype(f32), -1e-4)
    lam_im = lam_im.astype(f32)
    dt = jnp.exp(log_dt.astype(f32))[:, None]
    mag = jnp.exp(lam_re * dt)
    a_re = mag * jnp.cos(lam_im * dt)
    a_im = mag * jnp.sin(lam_im * dt)
    den = lam_re * lam_re + lam_im * lam_im
    num_re = a_re - 1.0
    f_re = (num_re * lam_re + a_im * lam_im) / den
    f_im = (a_im * lam_re - num_re * lam_im) / den
    b_re = b_re.astype(f32)
    b_im = b_im.astype(f32)
    bb_re = f_re[..., None] * b_re - f_im[..., None] * b_im
    bb_im = f_re[..., None] * b_im + f_im[..., None] * b_re
    return a_re, a_im, bb_re, bb_im


def _ssm_combine(left, right):
    a1r, a1i, b1r, b1i = left
    a2r, a2i, b2r, b2i = right
    ar = a2r * a1r - a2i * a1i
    ai = a2r * a1i + a2i * a1r
    br = a2r * b1r - a2i * b1i + b2r
    bi = a2r * b1i + a2i * b1r + b2i
    return ar, ai, br, bi


def s5_scan(u_tm, disc, s0):
    a_re, a_im, bb_re, bb_im = disc
    L = u_tm.shape[0]
    bu_re = jnp.einsum('lbgh,gph->lbgp', u_tm, bb_re)
    bu_im = jnp.einsum('lbgh,gph->lbgp', u_tm, bb_im)
    shape = (L, 1) + a_re.shape
    ar = jnp.broadcast_to(a_re, shape)
    ai = jnp.broadcast_to(a_im, shape)
    cum_re, cum_im, s_re, s_im = lax.associative_scan(_ssm_combine, (ar, ai, bu_re, bu_im), axis=0)
    if s0 is not None:
        s0_re, s0_im = s0
        s_re = s_re + cum_re * s0_re - cum_im * s0_im
        s_im = s_im + cum_re * s0_im + cum_im * s0_re
    return s_re, s_im


def s5_readout(s_re, s_im, c_re, c_im):
    return (jnp.einsum('lbgp,ghp->lbgh', s_re, c_re.astype(jnp.float32))
            - jnp.einsum('lbgp,ghp->lbgh', s_im, c_im.astype(jnp.float32)))


def s5_glu(y, w_glu):
    g = jax.nn.gelu(y)
    return g * jax.nn.sigmoid(g @ w_glu.astype(jnp.float32))


def s5_mixer(u_lat, u_ctx, lam_re, lam_im, log_dt, b_re, b_im, c_re, c_im, d_skip, w_glu, ctx_out):
    B, L, W = u_lat.shape
    Lc = u_ctx.shape[1]
    f32 = jnp.float32
    ul_f = u_lat.astype(f32)
    uc_f = u_ctx.astype(f32)
    ul = jnp.transpose(ul_f.reshape(B, L, S5_GROUPS, S5_GROUP), (1, 0, 2, 3))
    uc = jnp.transpose(uc_f.reshape(B, Lc, S5_GROUPS, S5_GROUP), (1, 0, 2, 3))
    d = d_skip.astype(f32)
    y_lat = d * ul_f
    y_ctx = d * uc_f if ctx_out else None
    for direction in range(2):
        rev = direction == 1
        disc = s5_discretise(lam_re[direction], lam_im[direction], log_dt[direction],
                             b_re[direction], b_im[direction])
        ucd = uc[::-1] if rev else uc
        uld = ul[::-1] if rev else ul
        sc_re, sc_im = s5_scan(ucd, disc, None)
        sl_re, sl_im = s5_scan(uld, disc, (sc_re[-1], sc_im[-1]))
        yl = s5_readout(sl_re, sl_im, c_re[direction], c_im[direction])
        yl = yl[::-1] if rev else yl
        y_lat = y_lat + jnp.transpose(yl, (1, 0, 2, 3)).reshape(B, L, W)
        if ctx_out:
            yc = s5_readout(sc_re, sc_im, c_re[direction], c_im[direction])
            yc = yc[::-1] if rev else yc
            y_ctx = y_ctx + jnp.transpose(yc, (1, 0, 2, 3)).reshape(B, Lc, W)
    out_lat = s5_glu(y_lat, w_glu).astype(u_lat.dtype)
    out_ctx = s5_glu(y_ctx, w_glu).astype(u_ctx.dtype) if ctx_out else None
    return out_lat, out_ctx


def na_mixer(q_l, k_l, v_l, q_c, k_c, v_c, rpb, ctx_out):
    f32 = jnp.float32
    B, L, NH, DH = q_l.shape
    rows = L // GRID_W
    win_r = min(NA_WIN_R, rows)
    win_c = NA_WIN_C
    scale = DH ** -0.5
    q_rows = jnp.arange(rows)
    r0 = jnp.clip(q_rows - win_r // 2, 0, rows - win_r)
    key_rows = r0[:, None] + jnp.arange(win_r)[None, :]
    cols = jnp.arange(GRID_W)
    c0 = jnp.clip(cols - win_c // 2, 0, GRID_W - win_c)
    col_ok = (cols[None, :] >= c0[:, None]) & (cols[None, :] < c0[:, None] + win_c)
    mask = jnp.broadcast_to(col_ok[:, None, :], (GRID_W, win_r, GRID_W)).reshape(GRID_W, win_r * GRID_W)
    dr = key_rows - q_rows[:, None] + (NA_WIN_R - 1)
    dc = jnp.clip(cols[None, :] - cols[:, None], -(win_c - 1), win_c - 1) + (win_c - 1)
    bias = rpb.astype(f32)[:, dr[:, None, :, None], dc[None, :, None, :]]
    bias = bias.reshape(NH, rows, GRID_W, win_r * GRID_W)
    qg = q_l.reshape(B, rows, GRID_W, NH, DH)
    k_blk = k_l.reshape(B, rows, GRID_W, NH, DH)[:, key_rows].reshape(B, rows, win_r * GRID_W, NH, DH)
    v_blk = v_l.reshape(B, rows, GRID_W, NH, DH)[:, key_rows].reshape(B, rows, win_r * GRID_W, NH, DH)
    s_lat = jnp.einsum('brqhd,brkhd->bhrqk', qg, k_blk).astype(f32) * scale + bias[None]
    s_lat = jnp.where(mask, s_lat, NEG_INF)
    s_ctx = jnp.einsum('brqhd,bkhd->bhrqk', qg, k_c).astype(f32) * scale
    p = jax.nn.softmax(jnp.concatenate([s_lat, s_ctx], axis=-1), axis=-1).astype(v_l.dtype)
    n_lat = win_r * GRID_W
    o = (jnp.einsum('bhrqk,brkhd->brqhd', p[..., :n_lat], v_blk)
         + jnp.einsum('bhrqk,bkhd->brqhd', p[..., n_lat:], v_c))
    out_lat = o.reshape(B, L, NH * DH)
    out_ctx = None
    if ctx_out:
        Lc = q_c.shape[1]
        s_cc = jnp.einsum('bqhd,bkhd->bhqk', q_c, k_c).astype(f32) * scale
        p_cc = jax.nn.softmax(s_cc, axis=-1).astype(v_c.dtype)
        out_ctx = jnp.einsum('bhqk,bkhd->bqhd', p_cc, v_c).reshape(B, Lc, NH * DH)
    return out_lat, out_ctx


def split_heads(z):
    B, L, _ = z.shape
    u = z[..., :S5_WIDTH]
    q, k, v = jnp.split(z[..., S5_WIDTH:], 3, axis=-1)
    shp = (B, L, NA_HEADS, NA_HEAD_DIM)
    return u, q.reshape(shp), k.reshape(shp), v.reshape(shp)


def even_mixer(hl, hc, w_in, w_out, lam_re, lam_im, log_dt, b_re, b_im, c_re, c_im, d_skip, w_glu, rpb, ctx_out):
    u_l, q_l, k_l, v_l = split_heads(hl @ w_in)
    u_c, q_c, k_c, v_c = split_heads(hc @ w_in)
    s5_l, s5_c = s5_mixer(u_l, u_c, lam_re, lam_im, log_dt, b_re, b_im, c_re, c_im, d_skip, w_glu, ctx_out)
    na_l, na_c = na_mixer(q_l, k_l, v_l, q_c, k_c, v_c, rpb, ctx_out)
    out_l = jnp.concatenate([s5_l, na_l], axis=-1) @ w_out
    out_c = jnp.concatenate([s5_c, na_c], axis=-1) @ w_out if ctx_out else None
    return out_l, out_c


def fourier_mix(h, w_f):
    hf = jnp.fft.fft2(h.astype(jnp.float32), axes=(1, 2), norm='ortho').real
    return hf.astype(h.dtype) @ w_f


def sqrelu_mlp(x, w1, w2):
    return jnp.square(jax.nn.relu(x @ w1)) @ w2


def setup_inputs(seed: int = 0) -> dict:
    key = jax.random.key(seed)
    ks = jax.random.split(key, 24)
    f32 = jnp.float32

    def nrm(k, shape, std):
        return jax.random.normal(k, shape, f32) * std

    G, P, H = S5_GROUPS, S5_STATE, S5_GROUP
    x = nrm(ks[0], (BATCH, SEQ, D_MODEL), 1.0)
    c = nrm(ks[1], (BATCH, D_MODEL), 1.0)
    ctx = nrm(ks[2], (BATCH, CTX_LEN, D_MODEL), 1.0)
    c_ctx = nrm(ks[3], (D_MODEL,), 1.0)
    w_mod = nrm(ks[4], (DEPTH, D_MODEL, N_MOD * D_MODEL), 0.5 * D_MODEL ** -0.5)
    b_mod = nrm(ks[5], (DEPTH, N_MOD * D_MODEL), 0.02)
    norm_g = 1.0 + nrm(ks[6], (DEPTH, 4, D_MODEL), 0.05)
    w_in = nrm(ks[7], (N_EVEN, D_MODEL, IN_WIDTH), D_MODEL ** -0.5)
    w_out_even = nrm(ks[8], (N_EVEN, MIX_WIDTH, D_MODEL), MIX_WIDTH ** -0.5)
    s5_lam_re = -0.5 + nrm(ks[9], (N_EVEN, 2, G, P), 0.01)
    s5_lam_im = math.pi * jnp.arange(P, dtype=f32) + nrm(ks[10], (N_EVEN, 2, G, P), 0.01)
    s5_log_dt = jax.random.uniform(ks[11], (N_EVEN, 2, G), f32, minval=math.log(1e-3), maxval=math.log(1e-1))
    s5_b_re = nrm(ks[12], (N_EVEN, 2, G, P, H), (2.0 * H) ** -0.5)
    s5_b_im = nrm(ks[13], (N_EVEN, 2, G, P, H), (2.0 * H) ** -0.5)
    s5_c_re = nrm(ks[14], (N_EVEN, 2, G, H, P), (2.0 * P) ** -0.5)
    s5_c_im = nrm(ks[15], (N_EVEN, 2, G, H, P), (2.0 * P) ** -0.5)
    s5_d = nrm(ks[16], (N_EVEN, S5_WIDTH), 1.0)
    s5_w_glu = nrm(ks[17], (N_EVEN, S5_WIDTH, S5_WIDTH), S5_WIDTH ** -0.5)
    na_rpb = nrm(ks[18], (N_EVEN, NA_HEADS, 2 * NA_WIN_R - 1, 2 * NA_WIN_C - 1), 0.1)
    w_fourier = nrm(ks[19], (N_ODD, D_MODEL, D_MODEL), D_MODEL ** -0.5)
    w_ff1 = nrm(ks[20], (DEPTH, D_MODEL, D_FF), D_MODEL ** -0.5)
    w_ff2 = nrm(ks[21], (DEPTH, D_FF, D_MODEL), D_FF ** -0.5)
    return {'x': x, 'c': c, 'ctx': ctx, 'c_ctx': c_ctx, 'w_mod': w_mod, 'b_mod': b_mod, 'norm_g': norm_g,
            'w_in': w_in, 'w_out_even': w_out_even, 's5_lam_re': s5_lam_re, 's5_lam_im': s5_lam_im,
            's5_log_dt': s5_log_dt, 's5_b_re': s5_b_re, 's5_b_im': s5_b_im, 's5_c_re': s5_c_re,
            's5_c_im': s5_c_im, 's5_d': s5_d, 's5_w_glu': s5_w_glu, 'na_rpb': na_rpb,
            'w_fourier': w_fourier, 'w_ff1': w_ff1, 'w_ff2': w_ff2}


def reference(x, c, ctx, c_ctx, w_mod, b_mod, norm_g, w_in, w_out_even, s5_lam_re, s5_lam_im, s5_log_dt,
              s5_b_re, s5_b_im, s5_c_re, s5_c_im, s5_d, s5_w_glu, na_rpb, w_fourier, w_ff1, w_ff2):
    last_ctx_layer = 2 * ((DEPTH - 1) // 2)
    h = x
    s = ctx
    c_act = jax.nn.silu(c)
    cc_act = jax.nn.silu(c_ctx)[None, :]
    for layer in range(DEPTH):
        need_ctx = layer <= last_ctx_layer
        upd_ctx = layer < last_ctx_layer
        mod_l = c_act @ w_mod[layer] + b_mod[layer]
        sh1, sc1, g1, sh2, sc2, g2 = jnp.split(mod_l, N_MOD, axis=-1)
        hl = modulate(rms_norm(h, norm_g[layer, 0]), sh1, sc1)
        hc = None
        if need_ctx:
            mod_c = cc_act @ w_mod[layer] + b_mod[layer]
            csh1, csc1, cg1, csh2, csc2, cg2 = jnp.split(mod_c, N_MOD, axis=-1)
            hc = modulate(rms_norm(s, norm_g[layer, 0]), csh1, csc1)
        if layer % 2 == 0:
            e = layer // 2
            out_l, out_c = even_mixer(hl, hc, w_in[e], w_out_even[e], s5_lam_re[e], s5_lam_im[e], s5_log_dt[e],
                                      s5_b_re[e], s5_b_im[e], s5_c_re[e], s5_c_im[e], s5_d[e], s5_w_glu[e],
                                      na_rpb[e], upd_ctx)
        else:
            o = layer // 2
            out_l = fourier_mix(hl, w_fourier[o])
            out_c = fourier_mix(hc, w_fourier[o]) if upd_ctx else None
        h = h + g1[:, None, :] * rms_norm(out_l, norm_g[layer, 1])
        hf = modulate(rms_norm(h, norm_g[layer, 2]), sh2, sc2)
        h = h + g2[:, None, :] * rms_norm(sqrelu_mlp(hf, w_ff1[layer], w_ff2[layer]), norm_g[layer, 3])
        if upd_ctx:
            s = s + cg1[:, None, :] * rms_norm(out_c, norm_g[layer, 1])
            sf = modulate(rms_norm(s, norm_g[layer, 2]), csh2, csc2)
            s = s + cg2[:, None, :] * rms_norm(sqrelu_mlp(sf, w_ff1[layer], w_ff2[layer]), norm_g[layer, 3])
    return h
```

```python
import functools
import math

import jax
import jax.numpy as jnp
from jax import lax
from jax.experimental import pallas as pl
from jax.experimental.pallas import tpu as pltpu

F32 = jnp.float32
BF16 = jnp.bfloat16
EPS = 1e-6
NEG_INF = -1e30

GRID_W = 64
S5_GROUP = 16
S5_STATE = 64
S5_T = 16
NA_HEAD_DIM = 64
NA_WIN_R = 8
NA_WIN_C = 16
NA_QROWS = 4
NA_KROWS = 12
N_MOD = 6

V7X_VMEM_LIMIT_BYTES = 56 * 1024 * 1024
HIGHEST = lax.Precision.HIGHEST


def _params(**kw):
    return pltpu.CompilerParams(vmem_limit_bytes=V7X_VMEM_LIMIT_BYTES, **kw)


def _resident(shape):
    nd = len(shape)
    return pl.BlockSpec(shape, lambda *_: (0,) * nd, pipeline_mode=pl.Buffered(1))


def _rms(x, g):
    ms = jnp.mean(x * x, axis=-1, keepdims=True)
    return x * lax.rsqrt(ms + EPS) * g


def _sigmoid(x):
    return 1.0 / (1.0 + jnp.exp(-x))


def _gelu_tanh(x):
    c = math.sqrt(2.0 / math.pi)
    return x * (0.5 * (1.0 + jnp.tanh(c * (x + 0.044715 * (x * x * x)))))


def _mod_kernel(a_ref, w_ref, b_ref, o_ref):
    a = a_ref[...]
    act = a * _sigmoid(a)
    o_ref[0] = jnp.dot(act.astype(BF16), w_ref[0].astype(BF16), preferred_element_type=F32) + b_ref[0]


def _modulation(a, w_mod, b_mod):
    depth, d, n = w_mod.shape
    rows = a.shape[0]
    tn = 1536
    return pl.pallas_call(
        _mod_kernel,
        out_shape=jax.ShapeDtypeStruct((depth, rows, n), F32),
        grid=(depth, n // tn),
        in_specs=[
            pl.BlockSpec((rows, d), lambda l, j: (0, 0)),
            pl.BlockSpec((1, d, tn), lambda l, j: (l, 0, j)),
            pl.BlockSpec((1, 1, tn), lambda l, j: (l, 0, j)),
        ],
        out_specs=pl.BlockSpec((1, rows, tn), lambda l, j: (l, 0, j)),
        compiler_params=_params(),
        name="adaln_mod",
    )(a, w_mod, b_mod.reshape(depth, 1, n))


def _pre_even_kernel(h_ref, mod_ref, g_ref, w_ref, u_ref, qkv_ref, *, s5_width):
    m = mod_ref[0]
    hl = _rms(h_ref[0], g_ref[0:1]) * (1.0 + m[1:2]) + m[0:1]
    z = jnp.dot(hl.astype(BF16), w_ref[...], preferred_element_type=F32)
    u_ref[0] = z[:, :s5_width]
    qkv_ref[0] = z[:, s5_width:].astype(BF16)


def _pre_even(h, mod, gains, w_in, tm):
    bsz, seq, d = h.shape
    n = w_in.shape[1]
    s5w = d // 2
    return pl.pallas_call(
        functools.partial(_pre_even_kernel, s5_width=s5w),
        out_shape=(jax.ShapeDtypeStruct((bsz, seq, s5w), F32),
                   jax.ShapeDtypeStruct((bsz, seq, n - s5w), BF16)),
        grid=(bsz, seq // tm),
        in_specs=[
            pl.BlockSpec((1, tm, d), lambda b, i: (b, i, 0)),
            pl.BlockSpec((1, 8, d), lambda b, i: (b, 0, 0)),
            _resident(gains.shape),
            _resident(w_in.shape),
        ],
        out_specs=(pl.BlockSpec((1, tm, s5w), lambda b, i: (b, i, 0)),
                   pl.BlockSpec((1, tm, n - s5w), lambda b, i: (b, i, 0))),
        compiler_params=_params(),
        name="pre_even",
    )(h, mod, gains, w_in)


def _pre_odd_kernel(h_ref, mod_ref, g_ref, w_ref, ab_ref, *, d):
    m = mod_ref[0]
    hl = _rms(h_ref[0], g_ref[0:1]) * (1.0 + m[1:2]) + m[0:1]
    z = jnp.dot(hl.astype(BF16), w_ref[...], preferred_element_type=F32)
    ab_ref[0] = z[:, :d].astype(BF16)
    ab_ref[1] = z[:, d:].astype(BF16)


def _pre_odd(h, mod, gains, w_cs, tm):
    bsz, seq, d = h.shape
    return pl.pallas_call(
        functools.partial(_pre_odd_kernel, d=d),
        out_shape=jax.ShapeDtypeStruct((2, seq, bsz * d), BF16),
        grid=(bsz, seq // tm),
        in_specs=[
            pl.BlockSpec((1, tm, d), lambda b, i: (b, i, 0)),
            pl.BlockSpec((1, 8, d), lambda b, i: (b, 0, 0)),
            _resident(gains.shape),
            _resident(w_cs.shape),
        ],
        out_specs=pl.BlockSpec((2, tm, d), lambda b, i: (0, i, b)),
        compiler_params=_params(),
        name="pre_odd",
    )(h, mod, gains, w_cs)


def _matmul_kernel(a_ref, b_ref, o_ref, acc_ref, *, nk):
    k = pl.program_id(2)

    @pl.when(k == 0)
    def _():
        acc_ref[...] = jnp.zeros_like(acc_ref)

    acc_ref[...] += jnp.dot(a_ref[...], b_ref[...], preferred_element_type=F32)

    @pl.when(k == nk - 1)
    def _():
        o_ref[...] = acc_ref[...].astype(o_ref.dtype)


def _matmul(a, b, tm, tn, tk, out_dtype):
    m, kdim = a.shape
    n = b.shape[1]
    nk = kdim // tk
    return pl.pallas_call(
        functools.partial(_matmul_kernel, nk=nk),
        out_shape=jax.ShapeDtypeStruct((m, n), out_dtype),
        grid=(m // tm, n // tn, nk),
        in_specs=[pl.BlockSpec((tm, tk), lambda i, j, k: (i, k)),
                  pl.BlockSpec((tk, tn), lambda i, j, k: (k, j))],
        out_specs=pl.BlockSpec((tm, tn), lambda i, j, k: (i, j)),
        scratch_shapes=[pltpu.VMEM((tm, tn), F32)],
        compiler_params=_params(),
        name="fnet_seq_dft",
    )(a, b)


def _dft_cos_sin(n, n0):
    n1 = n // n0
    k = jnp.arange(n, dtype=jnp.int32)[:, None]
    pa = (k * (jnp.arange(n1, dtype=jnp.int32)[None, :] * n0)) % n
    pb = (k * jnp.arange(n0, dtype=jnp.int32)[None, :]) % n
    w = 2.0 * math.pi / n
    ca, sa = jnp.cos(pa.astype(F32) * w), jnp.sin(pa.astype(F32) * w)
    cb, sb = jnp.cos(pb.astype(F32) * w), jnp.sin(pb.astype(F32) * w)
    cos = ca[:, :, None] * cb[:, None, :] - sa[:, :, None] * sb[:, None, :]
    sin = sa[:, :, None] * cb[:, None, :] + ca[:, :, None] * sb[:, None, :]
    return cos.reshape(n, n), sin.reshape(n, n)


def _dft_pair(n, n0, right):
    cos, sin = _dft_cos_sin(n, n0)
    scale = 1.0 / math.sqrt(n)
    sign = 1.0 if right else -1.0
    return (jnp.concatenate([cos, sign * sin], axis=1) * scale).astype(BF16)


def _s5_prepare(lam_re, lam_im, log_dt, b_re, b_im, c_re, c_im, d_skip):
    t_len, hh, pp = S5_T, S5_GROUP, S5_STATE
    ne, _, gg, _ = lam_re.shape
    lam_re = jnp.minimum(lam_re.astype(F32), -1e-4)
    lam_im = lam_im.astype(F32)
    dt = jnp.exp(log_dt.astype(F32))[..., None]
    mag = jnp.exp(lam_re * dt)
    a_re = mag * jnp.cos(lam_im * dt)
    a_im = mag * jnp.sin(lam_im * dt)
    den = lam_re * lam_re + lam_im * lam_im
    num_re = a_re - 1.0
    f_re = (num_re * lam_re + a_im * lam_im) / den
    f_im = (a_im * lam_re - num_re * lam_im) / den
    b_re = b_re.astype(F32)
    b_im = b_im.astype(F32)
    bb_re = f_re[..., None] * b_re - f_im[..., None] * b_im
    bb_im = f_re[..., None] * b_im + f_im[..., None] * b_re
    tau = jnp.arange(t_len + 1, dtype=F32).reshape(-1, 1, 1, 1, 1)
    pmag = jnp.exp(lam_re[None] * dt[None] * tau)
    p_re = pmag * jnp.cos(lam_im[None] * dt[None] * tau)
    p_im = pmag * jnp.sin(lam_im[None] * dt[None] * tau)
    ab_re = p_re[..., None] * bb_re[None] - p_im[..., None] * bb_im[None]
    ab_im = p_re[..., None] * bb_im[None] + p_im[..., None] * bb_re[None]
    c_re = c_re.astype(F32)
    c_im = c_im.astype(F32)
    kk = (jnp.einsum('edgkp,tedgph->tedgkh', c_re, ab_re, precision=HIGHEST)
          - jnp.einsum('edgkp,tedgph->tedgkh', c_im, ab_im, precision=HIGHEST))
    jj = jnp.arange(t_len)[:, None]
    tt = jnp.arange(t_len)[None, :]
    diff = tt - jj
    kf = kk[:, :, 0][jnp.clip(diff, 0, t_len - 1)] * (diff >= 0).astype(F32)[:, :, None, None, None, None]
    kb = kk[:, :, 1][jnp.clip(-diff, 0, t_len - 1)] * (diff <= 0).astype(F32)[:, :, None, None, None, None]
    toep = kf + kb
    toep = jnp.transpose(toep, (2, 3, 0, 5, 1, 4)).reshape(ne, gg, t_len * hh, t_len * hh)
    rev = jnp.arange(t_len - 1, -1, -1)

    def st(x, d, idx):
        return jnp.transpose(x[idx, :, d], (1, 2, 0, 4, 3)).reshape(ne, gg, t_len * hh, pp)

    fwd_idx, bwd_idx = rev, jnp.arange(t_len)
    st_blocks = [st(ab_re, 0, fwd_idx), st(ab_im, 0, fwd_idx), st(ab_re, 1, bwd_idx), st(ab_im, 1, bwd_idx)]
    e_re = c_re[None] * p_re[:, :, :, :, None, :] - c_im[None] * p_im[:, :, :, :, None, :]
    e_im = c_re[None] * p_im[:, :, :, :, None, :] + c_im[None] * p_re[:, :, :, :, None, :]

    def ot(x, d, idx):
        return jnp.transpose(x[idx, :, d], (1, 2, 4, 0, 3)).reshape(ne, gg, pp, t_len * hh)

    f_tau = jnp.arange(1, t_len + 1)
    b_tau = jnp.arange(t_len, 0, -1)
    out_blocks = [ot(e_re, 0, f_tau), -ot(e_im, 0, f_tau), ot(e_re, 1, b_tau), -ot(e_im, 1, b_tau)]

    gp = gg // 2
    w_y = t_len * hh

    def pair_rows(x):
        return x.reshape(ne, gp, 2, x.shape[2], x.shape[3])

    toep_p = pair_rows(toep)
    st_p = [pair_rows(x) for x in st_blocks]
    zy = jnp.zeros((ne, gp, w_y, w_y), F32)
    zs = jnp.zeros((ne, gp, w_y, pp), F32)
    row0 = jnp.concatenate([toep_p[:, :, 0], zy] + [y for x in st_p for y in (x[:, :, 0], zs)], axis=-1)
    row1 = jnp.concatenate([zy, toep_p[:, :, 1]] + [y for x in st_p for y in (zs, x[:, :, 1])], axis=-1)
    wbig = jnp.concatenate([row0, row1], axis=2).astype(BF16)

    zo = jnp.zeros((ne, gp, pp, w_y), F32)
    out_rows = []
    for x in out_blocks:
        xp = pair_rows(x)
        out_rows.append(jnp.concatenate([xp[:, :, 0], zo], axis=-1))
        out_rows.append(jnp.concatenate([zo, xp[:, :, 1]], axis=-1))
    wout = jnp.concatenate(out_rows, axis=2).astype(BF16)

    dec_rows = [p_re[t_len, :, 0], p_im[t_len, :, 0], p_re[t_len, :, 1], p_im[t_len, :, 1]]
    dec = jnp.stack([x.reshape(ne, gp, 2 * pp) for x in dec_rows], axis=2)
    dec = jnp.pad(dec, ((0, 0), (0, 0), (0, 4), (0, 0)))
    dsk = jnp.broadcast_to(d_skip.astype(F32).reshape(ne, gp, 2, 1, hh), (ne, gp, 2, t_len, hh))
    dsk = jnp.pad(dsk.reshape(ne, gp, 1, 2 * w_y), ((0, 0), (0, 0), (0, 7), (0, 0)))
    return wbig, wout, dec, dsk


def _s5_kernel(u_ref, wbig_ref, wout_ref, dec_ref, dsk_ref, y_ref, z_ref, sin_ref, *, n_ctx, n_chunks, bsz):
    rows = n_chunks * bsz
    tr = 128
    wy = 2 * S5_T * S5_GROUP
    sw = 2 * S5_STATE

    def stage1(i, carry):
        r = pl.multiple_of(i * tr, tr)
        ub = u_ref[0, pl.ds(r, tr), :].astype(BF16)
        z_ref[pl.ds(r, tr), :] = jnp.dot(ub, wbig_ref[0], preferred_element_type=F32)
        return carry

    lax.fori_loop(0, rows // tr, stage1, 0)

    dec = dec_ref[0]
    zero = jnp.zeros((bsz, sw), F32)

    def make_step(d):
        a_re = dec[2 * d:2 * d + 1]
        a_im = dec[2 * d + 1:2 * d + 2]
        lc = wy + 2 * d * sw
        sc = 2 * d * sw

        def step(c, carry):
            s_re, s_im = carry
            r = pl.multiple_of(c * bsz, bsz)
            sin_ref[pl.ds(r, bsz), sc:sc + sw] = s_re
            sin_ref[pl.ds(r, bsz), sc + sw:sc + 2 * sw] = s_im
            l_re = z_ref[pl.ds(r, bsz), lc:lc + sw]
            l_im = z_ref[pl.ds(r, bsz), lc + sw:lc + 2 * sw]
            return (a_re * s_re - a_im * s_im + l_re, a_re * s_im + a_im * s_re + l_im)

        return step

    fwd, bwd = make_step(0), make_step(1)
    lax.fori_loop(0, n_chunks, fwd, (zero, zero))
    carry = lax.fori_loop(0, n_ctx, lambda k, cr: bwd(n_ctx - 1 - k, cr), (zero, zero))
    lax.fori_loop(0, n_chunks - n_ctx, lambda k, cr: bwd(n_chunks - 1 - k, cr), carry)

    dsk = dsk_ref[0][0:1]

    def stage3(i, carry):
        r = pl.multiple_of(i * tr, tr)
        sb = sin_ref[pl.ds(r, tr), :].astype(BF16)
        y = (dsk * u_ref[0, pl.ds(r, tr), :] + z_ref[pl.ds(r, tr), 0:wy]
             + jnp.dot(sb, wout_ref[0], preferred_element_type=F32))
        y_ref[0, pl.ds(r, tr), :] = y
        return carry

    lax.fori_loop(0, rows // tr, stage3, 0)


def _s5_chunked(u_pairs, wbig, wout, dec, dsk, n_ctx, bsz):
    gp, rows, width = u_pairs.shape
    n_chunks = rows // bsz
    zc = wbig.shape[-1]
    return pl.pallas_call(
        functools.partial(_s5_kernel, n_ctx=n_ctx, n_chunks=n_chunks, bsz=bsz),
        out_shape=jax.ShapeDtypeStruct((gp, rows, width), F32),
        grid=(gp,),
        in_specs=[
            pl.BlockSpec((1, rows, width), lambda q: (q, 0, 0)),
            pl.BlockSpec((1,) + wbig.shape[1:], lambda q: (q, 0, 0)),
            pl.BlockSpec((1,) + wout.shape[1:], lambda q: (q, 0, 0)),
            pl.BlockSpec((1,) + dec.shape[1:], lambda q: (q, 0, 0)),
            pl.BlockSpec((1,) + dsk.shape[1:], lambda q: (q, 0, 0)),
        ],
        out_specs=pl.BlockSpec((1, rows, width), lambda q: (q, 0, 0)),
        scratch_shapes=[pltpu.VMEM((rows, zc), F32), pltpu.VMEM((rows, width), F32)],
        compiler_params=_params(),
        name="s5_chunked",
    )(u_pairs, wbig, wout, dec, dsk)


def _to_pairs(u):
    bsz, seq, width = u.shape
    gp = width // (2 * S5_GROUP)
    x = u.reshape(bsz, seq // S5_T, S5_T, gp, 2, S5_GROUP)
    x = jnp.transpose(x, (3, 1, 0, 4, 2, 5))
    return x.reshape(gp, seq // S5_T, bsz, 2 * S5_T * S5_GROUP)


def _from_pairs(y, bsz):
    gp, rows, _ = y.shape
    nch = rows // bsz
    x = y.reshape(gp, nch, bsz, 2, S5_T, S5_GROUP)
    x = jnp.transpose(x, (2, 1, 4, 0, 3, 5))
    return x.reshape(bsz, nch * S5_T, gp * 2 * S5_GROUP)


def _attend_pairs(q_ref, k_refs, v_refs, bias_ref, o_ref, n_biased):
    width = q_ref.shape[-1]
    kb = k_refs[0].shape[1]
    lane = lax.broadcasted_iota(jnp.int32, (1, 2 * NA_HEAD_DIM), 1)
    outs = []
    for hp in range(width // (2 * NA_HEAD_DIM)):
        cs = slice(hp * 2 * NA_HEAD_DIM, (hp + 1) * 2 * NA_HEAD_DIM)
        qp = q_ref[0, :, cs]
        ks = [r[0, :, cs] for r in k_refs]
        vs = [r[0, :, cs] for r in v_refs]
        res = []
        for e in range(2):
            sel = (lane // NA_HEAD_DIM) == e
            qe = jnp.where(sel, qp, jnp.zeros_like(qp))
            parts = []
            for i, kk in enumerate(ks):
                s = lax.dot_general(qe, kk, (((1,), (1,)), ((), ())), preferred_element_type=F32)
                if i < n_biased:
                    s = s + bias_ref[0, 2 * hp + e, :, i * kb:(i + 1) * kb]
                parts.append(s)
            m = parts[0].max(axis=-1, keepdims=True)
            for s in parts[1:]:
                m = jnp.maximum(m, s.max(axis=-1, keepdims=True))
            ps = [jnp.exp(s - m) for s in parts]
            den = ps[0].sum(axis=-1, keepdims=True)
            for p in ps[1:]:
                den = den + p.sum(axis=-1, keepdims=True)
            acc = jnp.dot(ps[0].astype(BF16), vs[0], preferred_element_type=F32)
            for p, vv in zip(ps[1:], vs[1:]):
                acc = acc + jnp.dot(p.astype(BF16), vv, preferred_element_type=F32)
            res.append(acc / den)
        outs.append(jnp.where(lane < NA_HEAD_DIM, res[0], res[1]))
    o_ref[0] = jnp.concatenate(outs, axis=-1).astype(o_ref.dtype)


def _na_lat_kernel(q_ref, k0, k1, k2, v0, v1, v2, kc, vc, bias_ref, o_ref):
    _attend_pairs(q_ref, (k0, k1, k2, kc), (v0, v1, v2, vc), bias_ref, o_ref, 3)


def _na_ctx_kernel(q_ref, kc, vc, o_ref):
    _attend_pairs(q_ref, (kc,), (vc,), None, o_ref, 0)


def _na_bias_table(rpb, rows):
    nh = rpb.shape[0]
    nblk = rows // NA_QROWS
    blocks = jnp.array([0, 1, nblk - 1], dtype=jnp.int32)
    kb0 = jnp.clip(blocks - 1, 0, nblk - NA_KROWS // NA_QROWS) * NA_QROWS
    qr = blocks[:, None] * NA_QROWS + jnp.arange(NA_QROWS)[None, :]
    r0 = jnp.clip(qr - NA_WIN_R // 2, 0, rows - NA_WIN_R)
    kr = kb0[:, None] + jnp.arange(NA_KROWS)[None, :]
    row_ok = (kr[:, None, :] >= r0[:, :, None]) & (kr[:, None, :] < r0[:, :, None] + NA_WIN_R)
    dr = jnp.clip(kr[:, None, :] - qr[:, :, None] + (NA_WIN_R - 1), 0, 2 * NA_WIN_R - 2)
    cols = jnp.arange(GRID_W)
    c0 = jnp.clip(cols - NA_WIN_C // 2, 0, GRID_W - NA_WIN_C)
    col_ok = (cols[None, :] >= c0[:, None]) & (cols[None, :] < c0[:, None] + NA_WIN_C)
    dc = jnp.clip(cols[None, :] - cols[:, None], -(NA_WIN_C - 1), NA_WIN_C - 1) + (NA_WIN_C - 1)
    bias = rpb.astype(F32)[:, dr[:, :, None, :, None], dc[None, None, :, None, :]]
    ok = row_ok[:, :, None, :, None] & col_ok[None, None, :, None, :]
    bias = jnp.where(ok[None], bias, NEG_INF)
    bias = jnp.transpose(bias, (1, 0, 2, 3, 4, 5))
    return bias.reshape(3, nh, NA_QROWS * GRID_W, NA_KROWS * GRID_W)


def _na_latent(qkv, qkv_c, bias):
    bsz, seq, w3 = qkv.shape
    width = w3 // 3
    lc = qkv_c.shape[1]
    qb = NA_QROWS * GRID_W
    nblk = seq // qb
    nkb = NA_KROWS // NA_QROWS

    def kv_spec(col, s):
        return pl.BlockSpec((1, qb, width),
                            lambda b, a: (b, jnp.clip(a - 1, 0, nblk - nkb) + s, col))

    def variant(a):
        return (a > 0).astype(jnp.int32) + (a == nblk - 1).astype(jnp.int32)

    return pl.pallas_call(
        _na_lat_kernel,
        out_shape=jax.ShapeDtypeStruct((bsz, seq, width), BF16),
        grid=(bsz, nblk),
        in_specs=[pl.BlockSpec((1, qb, width), lambda b, a: (b, a, 0))]
        + [kv_spec(1, s) for s in range(nkb)] + [kv_spec(2, s) for s in range(nkb)]
        + [pl.BlockSpec((1, lc, width), lambda b, a: (b, 0, 1)),
           pl.BlockSpec((1, lc, width), lambda b, a: (b, 0, 2)),
           pl.BlockSpec((1,) + bias.shape[1:], lambda b, a: (variant(a), 0, 0, 0))],
        out_specs=pl.BlockSpec((1, qb, width), lambda b, a: (b, a, 0)),
        compiler_params=_params(),
        name="na_latent",
    )(qkv, *([qkv] * (2 * nkb)), qkv_c, qkv_c, bias)


def _na_context(qkv_c):
    bsz, lc, w3 = qkv_c.shape
    width = w3 // 3
    return pl.pallas_call(
        _na_ctx_kernel,
        out_shape=jax.ShapeDtypeStruct((bsz, lc, width), BF16),
        grid=(bsz,),
        in_specs=[pl.BlockSpec((1, lc, width), lambda b: (b, 0, 0)),
                  pl.BlockSpec((1, lc, width), lambda b: (b, 0, 1)),
                  pl.BlockSpec((1, lc, width), lambda b: (b, 0, 2))],
        out_specs=pl.BlockSpec((1, lc, width), lambda b: (b, 0, 0)),
        compiler_params=_params(),
        name="na_context",
    )(qkv_c, qkv_c, qkv_c)


def _residual_mlp(out_l, h0, m, gains, w1_ref, w2_ref, ff_chunk):
    h1 = h0 + m[2:3] * _rms(out_l, gains[1:2])
    hf = (_rms(h1, gains[2:3]) * (1.0 + m[4:5]) + m[3:4]).astype(BF16)
    d_ff = w1_ref.shape[1]
    acc = None
    for kf in range(d_ff // ff_chunk):
        cs = slice(kf * ff_chunk, (kf + 1) * ff_chunk)
        hid = jnp.maximum(jnp.dot(hf, w1_ref[:, cs], preferred_element_type=F32), 0.0)
        part = jnp.dot((hid * hid).astype(BF16), w2_ref[cs, :], preferred_element_type=F32)
        acc = part if acc is None else acc + part
    return h1 + m[5:6] * _rms(acc, gains[3:4])


def _post_even_kernel(y_ref, na_ref, h_ref, mod_ref, g_ref, wglu_ref, wo_ref, w1_ref, w2_ref, o_ref, *, ff_chunk):
    g = _gelu_tanh(y_ref[0])
    gate = _sigmoid(jnp.dot(g.astype(BF16), wglu_ref[...], preferred_element_type=F32))
    s5 = (g * gate).astype(BF16)
    sw = s5.shape[1]
    out_l = (jnp.dot(s5, wo_ref[0:sw, :], preferred_element_type=F32)
             + jnp.dot(na_ref[0], wo_ref[sw:, :], preferred_element_type=F32))
    o_ref[0] = _residual_mlp(out_l, h_ref[0], mod_ref[0], g_ref[...], w1_ref, w2_ref, ff_chunk)


def _post_odd_kernel(mix_ref, h_ref, mod_ref, g_ref, wf_ref, w1_ref, w2_ref, o_ref, *, ff_chunk):
    out_l = jnp.dot(mix_ref[...], wf_ref[...], preferred_element_type=F32)
    o_ref[0] = _residual_mlp(out_l, h_ref[0], mod_ref[0], g_ref[...], w1_ref, w2_ref, ff_chunk)


def _post_even(y, na, h, mod, gains, w_glu, w_out, w1, w2, tm):
    bsz, seq, d = h.shape
    sw = y.shape[-1]
    tok = lambda w: pl.BlockSpec((1, tm, w), lambda b, i: (b, i, 0))
    return pl.pallas_call(
        functools.partial(_post_even_kernel, ff_chunk=1024),
        out_shape=jax.ShapeDtypeStruct(h.shape, F32),
        grid=(bsz, seq // tm),
        in_specs=[tok(sw), tok(na.shape[-1]), tok(d),
                  pl.BlockSpec((1, 8, d), lambda b, i: (b, 0, 0)),
                  _resident(gains.shape), _resident(w_glu.shape), _resident(w_out.shape),
                  _resident(w1.shape), _resident(w2.shape)],
        out_specs=tok(d),
        compiler_params=_params(),
        name="post_even",
    )(y, na, h, mod, gains, w_glu, w_out, w1, w2)


def _post_odd(mix, h, mod, gains, w_f, w1, w2, tm):
    bsz, seq, d = h.shape
    tok = pl.BlockSpec((1, tm, d), lambda b, i: (b, i, 0))
    return pl.pallas_call(
        functools.partial(_post_odd_kernel, ff_chunk=1024),
        out_shape=jax.ShapeDtypeStruct(h.shape, F32),
        grid=(bsz, seq // tm),
        in_specs=[pl.BlockSpec((tm, d), lambda b, i: (i, b)), tok,
                  pl.BlockSpec((1, 8, d), lambda b, i: (b, 0, 0)),
                  _resident(gains.shape), _resident(w_f.shape), _resident(w1.shape), _resident(w2.shape)],
        out_specs=tok,
        compiler_params=_params(),
        name="post_odd",
    )(mix, h, mod, gains, w_f, w1, w2)


def kernel(x, c, ctx, c_ctx, w_mod, b_mod, norm_g, w_in, w_out_even, s5_lam_re, s5_lam_im, s5_log_dt,
           s5_b_re, s5_b_im, s5_c_re, s5_c_im, s5_d, s5_w_glu, na_rpb, w_fourier, w_ff1, w_ff2):
    bsz, seq, d = x.shape
    lc = ctx.shape[1]
    depth = w_mod.shape[0]
    s5w = s5_d.shape[-1]
    last_ctx_layer = 2 * ((depth - 1) // 2)
    tm_lat, tm_ctx = 512, lc

    mod_rows = 16
    a = jnp.concatenate([c, c_ctx[None, :], jnp.zeros((mod_rows - bsz - 1, d), F32)], axis=0)
    mod = _modulation(a, w_mod, b_mod)
    pad_mod = lambda m: jnp.pad(m, ((0, 0), (0, 0), (0, 8 - N_MOD), (0, 0)))
    mod_l = pad_mod(mod[:, :bsz].reshape(depth, bsz, N_MOD, d))
    mod_c = pad_mod(jnp.broadcast_to(mod[:, bsz:bsz + 1].reshape(depth, 1, N_MOD, d), (depth, bsz, N_MOD, d)))
    gains = jnp.pad(norm_g.astype(F32), ((0, 0), (0, 4), (0, 0)))

    na_w = (w_in.shape[-1] - s5w) // 3
    qscale = jnp.concatenate([jnp.ones((s5w,), F32), jnp.full((na_w,), NA_HEAD_DIM ** -0.5, F32),
                              jnp.ones((2 * na_w,), F32)])
    w_in_b = (w_in * qscale).astype(BF16)
    w_out_b = w_out_even.astype(BF16)
    w_glu_b = s5_w_glu.astype(BF16)
    w_f_b = w_fourier.astype(BF16)
    w1_b = w_ff1.astype(BF16)
    w2_b = w_ff2.astype(BF16)

    wbig, wout, dec, dsk = _s5_prepare(s5_lam_re, s5_lam_im, s5_log_dt, s5_b_re, s5_b_im,
                                       s5_c_re, s5_c_im, s5_d)
    dft_d = _dft_pair(d, 32, right=True)
    dft_l = _dft_pair(seq, 64, right=False)
    dft_c = _dft_pair(lc, 16, right=False)

    h, s = x, ctx
    for layer in range(depth):
        need_ctx = layer <= last_ctx_layer
        upd_ctx = layer < last_ctx_layer
        g_l = gains[layer]
        if layer % 2 == 0:
            e = layer // 2
            u_l, qkv_l = _pre_even(h, mod_l[layer], g_l, w_in_b[e], tm_lat)
            u_c, qkv_c = _pre_even(s, mod_c[layer], g_l, w_in_b[e], tm_ctx)
            n_ctx = lc // S5_T
            u_pairs = jnp.concatenate([_to_pairs(u_c), _to_pairs(u_l)], axis=1)
            u_pairs = u_pairs.reshape(u_pairs.shape[0], -1, u_pairs.shape[-1])
            y_pairs = _s5_chunked(u_pairs, wbig[e], wout[e], dec[e], dsk[e], n_ctx, bsz)
            y_l = _from_pairs(y_pairs[:, n_ctx * bsz:], bsz)
            bias = _na_bias_table(na_rpb[e], seq // GRID_W)
            na_l = _na_latent(qkv_l, qkv_c, bias)
            h_new = _post_even(y_l, na_l, h, mod_l[layer], g_l, w_glu_b[e], w_out_b[e],
                               w1_b[layer], w2_b[layer], tm_lat)
            if upd_ctx:
                y_c = _from_pairs(y_pairs[:, :n_ctx * bsz], bsz)
                na_c = _na_context(qkv_c)
                s = _post_even(y_c, na_c, s, mod_c[layer], g_l, w_glu_b[e], w_out_b[e],
                               w1_b[layer], w2_b[layer], tm_ctx)
            h = h_new
        else:
            o = layer // 2
            ab = _pre_odd(h, mod_l[layer], g_l, dft_d, tm_lat)
            mix = _matmul(dft_l, ab.reshape(2 * seq, bsz * d), 1024, 2048, 1024, BF16)
            h_new = _post_odd(mix, h, mod_l[layer], g_l, w_f_b[o], w1_b[layer], w2_b[layer], tm_lat)
            if upd_ctx:
                ab_c = _pre_odd(s, mod_c[layer], g_l, dft_d, tm_ctx)
                mix_c = _matmul(dft_c, ab_c.reshape(2 * lc, bsz * d), lc, 2048, 2 * lc, BF16)
                s = _post_odd(mix_c, s, mod_c[layer], g_l, w_f_b[o], w1_b[layer], w2_b[layer], tm_ctx)
            h = h_new
    return h
```

```python
import functools
import math

import jax
import jax.numpy as jnp
import numpy as np
from jax import lax
from jax.experimental import pallas as pl
from jax.experimental.pallas import tpu as pltpu

F32 = jnp.float32
BF16 = jnp.bfloat16
EPS = 1e-6
NEG_INF = -1e30

GRID_W = 64
S5_GROUP = 16
S5_STATE = 64
S5_T = 16
S5_PAIR = 2 * S5_GROUP
LANES = 128
NA_HEAD_DIM = 64
NA_WIN_R = 8
NA_WIN_C = 16
NA_QROWS = 4
NA_KROWS = 12
N_MOD = 6

V7X_VMEM_LIMIT_BYTES = 56 * 1024 * 1024
HIGHEST = lax.Precision.HIGHEST


def _params(**kw):
    return pltpu.CompilerParams(vmem_limit_bytes=V7X_VMEM_LIMIT_BYTES, **kw)


def _resident(shape):
    nd = len(shape)
    return pl.BlockSpec(shape, lambda *_: (0,) * nd, pipeline_mode=pl.Buffered(1))


def _rms(x, g):
    ms = jnp.mean(x * x, axis=-1, keepdims=True)
    return x * lax.rsqrt(ms + EPS) * g


def _sigmoid(x):
    return 1.0 / (1.0 + jnp.exp(-x))


def _gelu_tanh(x):
    c = math.sqrt(2.0 / math.pi)
    return x * (0.5 * (1.0 + jnp.tanh(c * (x + 0.044715 * (x * x * x)))))


def _mod_kernel(a_ref, w_ref, b_ref, o_ref):
    a = a_ref[...]
    act = a * _sigmoid(a)
    o_ref[0] = jnp.dot(act.astype(BF16), w_ref[0].astype(BF16), preferred_element_type=F32) + b_ref[0]


def _modulation(a, w_mod, b_mod):
    depth, d, n = w_mod.shape
    rows = a.shape[0]
    tn = 1536
    return pl.pallas_call(
        _mod_kernel,
        out_shape=jax.ShapeDtypeStruct((depth, rows, n), F32),
        grid=(depth, n // tn),
        in_specs=[
            pl.BlockSpec((rows, d), lambda l, j: (0, 0)),
            pl.BlockSpec((1, d, tn), lambda l, j: (l, 0, j)),
            pl.BlockSpec((1, 1, tn), lambda l, j: (l, 0, j)),
        ],
        out_specs=pl.BlockSpec((1, rows, tn), lambda l, j: (l, 0, j)),
        compiler_params=_params(),
        name="adaln_mod",
    )(a, w_mod, b_mod.reshape(depth, 1, n))


def _lane_quarter(shape):
    return lax.broadcasted_iota(jnp.int32, shape, 1) // S5_PAIR


def _tokens_to_pairs(tok_ref, pair_ref, n_chunks):
    per = LANES // S5_PAIR
    quarter = _lane_quarter((n_chunks, LANES))
    for blk in range(tok_ref.shape[0]):
        for m in range(S5_T // per):
            src = [tok_ref[blk, pl.ds(per * m + jj, n_chunks, stride=S5_T), :] for jj in range(per)]
            for qq in range(per):
                dest = None
                for jj in range(per):
                    shift = ((jj - qq) % per) * S5_PAIR
                    r = src[jj] if shift == 0 else pltpu.roll(src[jj], shift, axis=1)
                    dest = r if dest is None else jnp.where(quarter == jj, r, dest)
                pair_ref[per * blk + qq, :, m * LANES:(m + 1) * LANES] = dest


def _pairs_to_tokens(pair_ref, tok_ref, n_chunks):
    per = LANES // S5_PAIR
    quarter = _lane_quarter((n_chunks, LANES))
    for blk in range(tok_ref.shape[0]):
        for m in range(S5_T // per):
            src = [pair_ref[per * blk + qq, :, m * LANES:(m + 1) * LANES] for qq in range(per)]
            for tt in range(per):
                dest = None
                for qq in range(per):
                    shift = ((qq - tt) % per) * S5_PAIR
                    r = src[qq] if shift == 0 else pltpu.roll(src[qq], shift, axis=1)
                    dest = r if dest is None else jnp.where(quarter == qq, r, dest)
                tok_ref[blk, pl.ds(per * m + tt, n_chunks, stride=S5_T), :] = dest


def _pre_even_kernel(h_ref, mod_ref, g_ref, w_ref, up_ref, qkv_ref, us_ref, *, s5_width):
    m = mod_ref[0]
    hl = _rms(h_ref[0], g_ref[0:1]) * (1.0 + m[1:2]) + m[0:1]
    z = jnp.dot(hl.astype(BF16), w_ref[...], preferred_element_type=F32)
    for blk in range(s5_width // LANES):
        us_ref[blk] = z[:, blk * LANES:(blk + 1) * LANES]
    qkv_ref[0] = z[:, s5_width:].astype(BF16)
    _tokens_to_pairs(us_ref, up_ref, us_ref.shape[1] // S5_T)


def _pre_even(h, mod, gains, w_in, tm):
    bsz, seq, d = h.shape
    n = w_in.shape[1]
    s5w = d // 2
    npair = s5w // S5_PAIR
    pw = S5_T * S5_PAIR
    return pl.pallas_call(
        functools.partial(_pre_even_kernel, s5_width=s5w),
        out_shape=(jax.ShapeDtypeStruct((npair, seq // S5_T, bsz * pw), F32),
                   jax.ShapeDtypeStruct((bsz, seq, n - s5w), BF16)),
        grid=(bsz, seq // tm),
        in_specs=[
            pl.BlockSpec((1, tm, d), lambda b, i: (b, i, 0)),
            pl.BlockSpec((1, 8, d), lambda b, i: (b, 0, 0)),
            _resident(gains.shape),
            _resident(w_in.shape),
        ],
        out_specs=(pl.BlockSpec((npair, tm // S5_T, pw), lambda b, i: (0, i, b)),
                   pl.BlockSpec((1, tm, n - s5w), lambda b, i: (b, i, 0))),
        scratch_shapes=[pltpu.VMEM((s5w // LANES, tm, LANES), F32)],
        compiler_params=_params(),
        name="pre_even",
    )(h, mod, gains, w_in)


def _pre_odd_kernel(h_ref, mod_ref, g_ref, w_ref, ab_ref, *, d):
    m = mod_ref[0]
    hl = _rms(h_ref[0], g_ref[0:1]) * (1.0 + m[1:2]) + m[0:1]
    z = jnp.dot(hl.astype(BF16), w_ref[...], preferred_element_type=F32)
    ab_ref[0] = z[:, :d].astype(BF16)
    ab_ref[1] = z[:, d:].astype(BF16)


def _pre_odd(h, mod, gains, w_cs, tm):
    bsz, seq, d = h.shape
    return pl.pallas_call(
        functools.partial(_pre_odd_kernel, d=d),
        out_shape=jax.ShapeDtypeStruct((2, seq, bsz * d), BF16),
        grid=(bsz, seq // tm),
        in_specs=[
            pl.BlockSpec((1, tm, d), lambda b, i: (b, i, 0)),
            pl.BlockSpec((1, 8, d), lambda b, i: (b, 0, 0)),
            _resident(gains.shape),
            _resident(w_cs.shape),
        ],
        out_specs=pl.BlockSpec((2, tm, d), lambda b, i: (0, i, b)),
        compiler_params=_params(),
        name="pre_odd",
    )(h, mod, gains, w_cs)


def _matmul_kernel(a_ref, b_ref, o_ref, acc_ref, *, nk):
    k = pl.program_id(2)

    @pl.when(k == 0)
    def _():
        acc_ref[...] = jnp.zeros_like(acc_ref)

    acc_ref[...] += jnp.dot(a_ref[...], b_ref[...], preferred_element_type=F32)

    @pl.when(k == nk - 1)
    def _():
        o_ref[...] = acc_ref[...].astype(o_ref.dtype)


def _matmul(a, b, tm, tn, tk, out_dtype):
    m, kdim = a.shape
    n = b.shape[1]
    nk = kdim // tk
    return pl.pallas_call(
        functools.partial(_matmul_kernel, nk=nk),
        out_shape=jax.ShapeDtypeStruct((m, n), out_dtype),
        grid=(m // tm, n // tn, nk),
        in_specs=[pl.BlockSpec((tm, tk), lambda i, j, k: (i, k)),
                  pl.BlockSpec((tk, tn), lambda i, j, k: (k, j))],
        out_specs=pl.BlockSpec((tm, tn), lambda i, j, k: (i, j)),
        scratch_shapes=[pltpu.VMEM((tm, tn), F32)],
        compiler_params=_params(),
        name="fnet_seq_dft",
    )(a, b)


def _dft_cos_sin(n, n0):
    n1 = n // n0
    k = jnp.arange(n, dtype=jnp.int32)[:, None]
    pa = (k * (jnp.arange(n1, dtype=jnp.int32)[None, :] * n0)) % n
    pb = (k * jnp.arange(n0, dtype=jnp.int32)[None, :]) % n
    w = 2.0 * math.pi / n
    ca, sa = jnp.cos(pa.astype(F32) * w), jnp.sin(pa.astype(F32) * w)
    cb, sb = jnp.cos(pb.astype(F32) * w), jnp.sin(pb.astype(F32) * w)
    cos = ca[:, :, None] * cb[:, None, :] - sa[:, :, None] * sb[:, None, :]
    sin = sa[:, :, None] * cb[:, None, :] + ca[:, :, None] * sb[:, None, :]
    return cos.reshape(n, n), sin.reshape(n, n)


def _dft_pair(n, n0, right):
    cos, sin = _dft_cos_sin(n, n0)
    scale = 1.0 / math.sqrt(n)
    sign = 1.0 if right else -1.0
    return (jnp.concatenate([cos, sign * sin], axis=1) * scale).astype(BF16)


def _s5_prepare(lam_re, lam_im, log_dt, b_re, b_im, c_re, c_im, d_skip):
    t_len, hh, pp = S5_T, S5_GROUP, S5_STATE
    ne, _, gg, _ = lam_re.shape
    lam_re = jnp.minimum(lam_re.astype(F32), -1e-4)
    lam_im = lam_im.astype(F32)
    dt = jnp.exp(log_dt.astype(F32))[..., None]
    mag = jnp.exp(lam_re * dt)
    a_re = mag * jnp.cos(lam_im * dt)
    a_im = mag * jnp.sin(lam_im * dt)
    den = lam_re * lam_re + lam_im * lam_im
    num_re = a_re - 1.0
    f_re = (num_re * lam_re + a_im * lam_im) / den
    f_im = (a_im * lam_re - num_re * lam_im) / den
    b_re = b_re.astype(F32)
    b_im = b_im.astype(F32)
    bb_re = f_re[..., None] * b_re - f_im[..., None] * b_im
    bb_im = f_re[..., None] * b_im + f_im[..., None] * b_re
    tau = jnp.arange(t_len + 1, dtype=F32).reshape(-1, 1, 1, 1, 1)
    pmag = jnp.exp(lam_re[None] * dt[None] * tau)
    p_re = pmag * jnp.cos(lam_im[None] * dt[None] * tau)
    p_im = pmag * jnp.sin(lam_im[None] * dt[None] * tau)
    ab_re = p_re[..., None] * bb_re[None] - p_im[..., None] * bb_im[None]
    ab_im = p_re[..., None] * bb_im[None] + p_im[..., None] * bb_re[None]
    c_re = c_re.astype(F32)
    c_im = c_im.astype(F32)
    kk = (jnp.einsum('edgkp,tedgph->tedgkh', c_re, ab_re, precision=HIGHEST)
          - jnp.einsum('edgkp,tedgph->tedgkh', c_im, ab_im, precision=HIGHEST))
    jj = jnp.arange(t_len)[:, None]
    tt = jnp.arange(t_len)[None, :]
    diff = tt - jj
    kf = kk[:, :, 0][jnp.clip(diff, 0, t_len - 1)] * (diff >= 0).astype(F32)[:, :, None, None, None, None]
    kb = kk[:, :, 1][jnp.clip(-diff, 0, t_len - 1)] * (diff <= 0).astype(F32)[:, :, None, None, None, None]
    toep = kf + kb
    toep = jnp.transpose(toep, (2, 3, 0, 5, 1, 4)).reshape(ne, gg, t_len * hh, t_len * hh)
    rev = jnp.arange(t_len - 1, -1, -1)

    def st(x, d, idx):
        return jnp.transpose(x[idx, :, d], (1, 2, 0, 4, 3)).reshape(ne, gg, t_len * hh, pp)

    fwd_idx, bwd_idx = rev, jnp.arange(t_len)
    st_blocks = [st(ab_re, 0, fwd_idx), st(ab_im, 0, fwd_idx), st(ab_re, 1, bwd_idx), st(ab_im, 1, bwd_idx)]
    e_re = c_re[None] * p_re[:, :, :, :, None, :] - c_im[None] * p_im[:, :, :, :, None, :]
    e_im = c_re[None] * p_im[:, :, :, :, None, :] + c_im[None] * p_re[:, :, :, :, None, :]

    def ot(x, d, idx):
        return jnp.transpose(x[idx, :, d], (1, 2, 4, 0, 3)).reshape(ne, gg, pp, t_len * hh)

    f_tau = jnp.arange(1, t_len + 1)
    b_tau = jnp.arange(t_len, 0, -1)
    out_blocks = [ot(e_re, 0, f_tau), -ot(e_im, 0, f_tau), ot(e_re, 1, b_tau), -ot(e_im, 1, b_tau)]

    gp = gg // 2
    pw = t_len * 2 * hh
    eye = jnp.eye(2, dtype=F32)
    tp = jnp.transpose(toep.reshape(ne, gp, 2, t_len, hh, t_len, hh), (0, 1, 3, 2, 4, 5, 6))
    tp = tp[:, :, :, :, :, :, None, :] * eye[None, None, None, :, None, None, :, None]
    cols = [tp.reshape(ne, gp, pw, pw)]
    for x in st_blocks:
        sp = jnp.transpose(x.reshape(ne, gp, 2, t_len, hh, pp), (0, 1, 3, 2, 4, 5))
        sp = sp[:, :, :, :, :, None, :] * eye[None, None, None, :, None, :, None]
        cols.append(sp.reshape(ne, gp, pw, 2 * pp))
    wbig = jnp.concatenate(cols, axis=-1).astype(BF16)

    rows = []
    for x in out_blocks:
        op = x.reshape(ne, gp, 2, pp, t_len, hh)
        op = op[:, :, :, :, :, None, :] * eye[None, None, :, None, None, :, None]
        rows.append(op.reshape(ne, gp, 2 * pp, pw))
    wout = jnp.concatenate(rows, axis=2).astype(BF16)

    dec_rows = [p_re[t_len, :, 0], p_im[t_len, :, 0], p_re[t_len, :, 1], p_im[t_len, :, 1]]
    dec = jnp.stack([x.reshape(ne, gp, 2 * pp) for x in dec_rows], axis=2)
    dec = jnp.pad(dec, ((0, 0), (0, 0), (0, 4), (0, 0)))
    dsk = jnp.broadcast_to(d_skip.astype(F32).reshape(ne, gp, 1, 2 * hh), (ne, gp, t_len, 2 * hh))
    dsk = jnp.pad(dsk.reshape(ne, gp, 1, pw), ((0, 0), (0, 0), (0, 7), (0, 0)))
    return wbig, wout, dec, dsk


def _s5_kernel(uc_ref, ul_ref, wbig_ref, wout_ref, dec_ref, dsk_ref, yc_ref, yl_ref, z_ref, sin_ref, *, bsz):
    rows_c, rows_l = uc_ref.shape[1], ul_ref.shape[1]
    n_ctx, n_chunks = rows_c // bsz, (rows_c + rows_l) // bsz
    tr = 128
    wy = S5_T * S5_PAIR
    sw = 2 * S5_STATE
    parts = ((uc_ref, yc_ref, 0, rows_c), (ul_ref, yl_ref, rows_c, rows_l))

    for u_ref, _, base, nrows in parts:
        def stage1(i, carry, u_ref=u_ref, base=base):
            r = pl.multiple_of(i * tr, tr)
            ub = u_ref[0, pl.ds(r, tr), :].astype(BF16)
            z_ref[pl.ds(base + r, tr), :] = jnp.dot(ub, wbig_ref[0], preferred_element_type=F32)
            return carry

        lax.fori_loop(0, nrows // tr, stage1, 0)

    dec = dec_ref[0]
    zero = jnp.zeros((bsz, sw), F32)

    def make_step(d):
        a_re = dec[2 * d:2 * d + 1]
        a_im = dec[2 * d + 1:2 * d + 2]
        lc = wy + 2 * d * sw
        sc = 2 * d * sw

        def step(c, carry):
            s_re, s_im = carry
            r = pl.multiple_of(c * bsz, bsz)
            sin_ref[pl.ds(r, bsz), sc:sc + sw] = s_re
            sin_ref[pl.ds(r, bsz), sc + sw:sc + 2 * sw] = s_im
            l_re = z_ref[pl.ds(r, bsz), lc:lc + sw]
            l_im = z_ref[pl.ds(r, bsz), lc + sw:lc + 2 * sw]
            return (a_re * s_re - a_im * s_im + l_re, a_re * s_im + a_im * s_re + l_im)

        return step

    fwd, bwd = make_step(0), make_step(1)
    lax.fori_loop(0, n_chunks, fwd, (zero, zero))
    carry = lax.fori_loop(0, n_ctx, lambda k, cr: bwd(n_ctx - 1 - k, cr), (zero, zero))
    lax.fori_loop(0, n_chunks - n_ctx, lambda k, cr: bwd(n_chunks - 1 - k, cr), carry)

    dsk = dsk_ref[0][0:1]

    for u_ref, y_ref, base, nrows in parts:
        def stage3(i, carry, u_ref=u_ref, y_ref=y_ref, base=base):
            r = pl.multiple_of(i * tr, tr)
            sb = sin_ref[pl.ds(base + r, tr), :].astype(BF16)
            y = (dsk * u_ref[0, pl.ds(r, tr), :] + z_ref[pl.ds(base + r, tr), 0:wy]
                 + jnp.dot(sb, wout_ref[0], preferred_element_type=F32))
            y_ref[0, pl.ds(r, tr), :] = y
            return carry

        lax.fori_loop(0, nrows // tr, stage3, 0)


def _s5_chunked(u_ctx, u_lat, wbig, wout, dec, dsk, bsz):
    gp, rows_c, width = u_ctx.shape
    rows_l = u_lat.shape[1]
    zc = wbig.shape[-1]
    per_pair = lambda shape: pl.BlockSpec((1,) + shape[1:], lambda q: (q, 0, 0))
    return pl.pallas_call(
        functools.partial(_s5_kernel, bsz=bsz),
        out_shape=(jax.ShapeDtypeStruct(u_ctx.shape, F32), jax.ShapeDtypeStruct(u_lat.shape, F32)),
        grid=(gp,),
        in_specs=[per_pair(u_ctx.shape), per_pair(u_lat.shape), per_pair(wbig.shape), per_pair(wout.shape),
                  per_pair(dec.shape), per_pair(dsk.shape)],
        out_specs=(per_pair(u_ctx.shape), per_pair(u_lat.shape)),
        scratch_shapes=[pltpu.VMEM((rows_c + rows_l, zc), F32), pltpu.VMEM((rows_c + rows_l, width), F32)],
        compiler_params=_params(),
        name="s5_chunked",
    )(u_ctx, u_lat, wbig, wout, dec, dsk)


def _attend_pairs(q_ref, k_refs, v_refs, bias_ref, o_ref, n_biased):
    width = q_ref.shape[-1]
    kb = k_refs[0].shape[1]
    lane = lax.broadcasted_iota(jnp.int32, (1, 2 * NA_HEAD_DIM), 1)
    outs = []
    for hp in range(width // (2 * NA_HEAD_DIM)):
        cs = slice(hp * 2 * NA_HEAD_DIM, (hp + 1) * 2 * NA_HEAD_DIM)
        qp = q_ref[0, :, cs]
        ks = [r[0, :, cs] for r in k_refs]
        vs = [r[0, :, cs] for r in v_refs]
        res = []
        for e in range(2):
            sel = (lane // NA_HEAD_DIM) == e
            qe = jnp.where(sel, qp, jnp.zeros_like(qp))
            parts = []
            for i, kk in enumerate(ks):
                s = lax.dot_general(qe, kk, (((1,), (1,)), ((), ())), preferred_element_type=F32)
                if i < n_biased:
                    s = s + bias_ref[0, 2 * hp + e, :, i * kb:(i + 1) * kb]
                parts.append(s)
            m = parts[0].max(axis=-1, keepdims=True)
            for s in parts[1:]:
                m = jnp.maximum(m, s.max(axis=-1, keepdims=True))
            ps = [jnp.exp(s - m) for s in parts]
            den = ps[0].sum(axis=-1, keepdims=True)
            for p in ps[1:]:
                den = den + p.sum(axis=-1, keepdims=True)
            acc = jnp.dot(ps[0].astype(BF16), vs[0], preferred_element_type=F32)
            for p, vv in zip(ps[1:], vs[1:]):
                acc = acc + jnp.dot(p.astype(BF16), vv, preferred_element_type=F32)
            res.append(acc / den)
        outs.append(jnp.where(lane < NA_HEAD_DIM, res[0], res[1]))
    o_ref[0] = jnp.concatenate(outs, axis=-1).astype(o_ref.dtype)


def _na_lat_kernel(q_ref, k0, k1, k2, v0, v1, v2, kc, vc, bias_ref, o_ref):
    _attend_pairs(q_ref, (k0, k1, k2, kc), (v0, v1, v2, vc), bias_ref, o_ref, 3)


def _na_ctx_kernel(q_ref, kc, vc, o_ref):
    _attend_pairs(q_ref, (kc,), (vc,), None, o_ref, 0)


def _na_bias_table(rpb, rows):
    nh = rpb.shape[0]
    nblk = rows // NA_QROWS
    blocks = np.array([0, 1, nblk - 1])
    kb0 = np.clip(blocks - 1, 0, nblk - NA_KROWS // NA_QROWS) * NA_QROWS
    qr = blocks[:, None] * NA_QROWS + np.arange(NA_QROWS)[None, :]
    r0 = np.clip(qr - NA_WIN_R // 2, 0, rows - NA_WIN_R)
    kr = kb0[:, None] + np.arange(NA_KROWS)[None, :]
    row_ok = (kr[:, None, :] >= r0[:, :, None]) & (kr[:, None, :] < r0[:, :, None] + NA_WIN_R)
    dr = np.clip(kr[:, None, :] - qr[:, :, None] + (NA_WIN_R - 1), 0, 2 * NA_WIN_R - 2)
    cols = np.arange(GRID_W)
    c0 = np.clip(cols - NA_WIN_C // 2, 0, GRID_W - NA_WIN_C)
    col_ok = (cols[None, :] >= c0[:, None]) & (cols[None, :] < c0[:, None] + NA_WIN_C)
    dc = np.clip(cols[None, :] - cols[:, None], -(NA_WIN_C - 1), NA_WIN_C - 1) + (NA_WIN_C - 1)
    sel_r = (np.arange(2 * NA_WIN_R - 1)[None, None, None, :] == dr[..., None]).astype(np.float32)
    sel_c = (np.arange(2 * NA_WIN_C - 1)[:, None, None] == dc[None]).astype(np.float32)
    by_col = jnp.einsum('hrc,cqk->hrqk', rpb.astype(F32), sel_c, precision=HIGHEST)
    bias = jnp.einsum('vijr,hrqk->vhiqjk', sel_r, by_col, precision=HIGHEST)
    ok = row_ok[:, None, :, None, :, None] & col_ok[None, None, None, :, None, :]
    bias = jnp.where(ok, bias, NEG_INF)
    return bias.reshape(3, nh, NA_QROWS * GRID_W, NA_KROWS * GRID_W)


def _na_latent(qkv, qkv_c, bias):
    bsz, seq, w3 = qkv.shape
    width = w3 // 3
    lc = qkv_c.shape[1]
    qb = NA_QROWS * GRID_W
    nblk = seq // qb
    nkb = NA_KROWS // NA_QROWS

    def kv_spec(col, s):
        return pl.BlockSpec((1, qb, width),
                            lambda b, a: (b, jnp.clip(a - 1, 0, nblk - nkb) + s, col))

    def variant(a):
        return (a > 0).astype(jnp.int32) + (a == nblk - 1).astype(jnp.int32)

    return pl.pallas_call(
        _na_lat_kernel,
        out_shape=jax.ShapeDtypeStruct((bsz, seq, width), BF16),
        grid=(bsz, nblk),
        in_specs=[pl.BlockSpec((1, qb, width), lambda b, a: (b, a, 0))]
        + [kv_spec(1, s) for s in range(nkb)] + [kv_spec(2, s) for s in range(nkb)]
        + [pl.BlockSpec((1, lc, width), lambda b, a: (b, 0, 1)),
           pl.BlockSpec((1, lc, width), lambda b, a: (b, 0, 2)),
           pl.BlockSpec((1,) + bias.shape[1:], lambda b, a: (variant(a), 0, 0, 0))],
        out_specs=pl.BlockSpec((1, qb, width), lambda b, a: (b, a, 0)),
        compiler_params=_params(),
        name="na_latent",
    )(qkv, *([qkv] * (2 * nkb)), qkv_c, qkv_c, bias)


def _na_context(qkv_c):
    bsz, lc, w3 = qkv_c.shape
    width = w3 // 3
    return pl.pallas_call(
        _na_ctx_kernel,
        out_shape=jax.ShapeDtypeStruct((bsz, lc, width), BF16),
        grid=(bsz,),
        in_specs=[pl.BlockSpec((1, lc, width), lambda b: (b, 0, 0)),
                  pl.BlockSpec((1, lc, width), lambda b: (b, 0, 1)),
                  pl.BlockSpec((1, lc, width), lambda b: (b, 0, 2))],
        out_specs=pl.BlockSpec((1, lc, width), lambda b: (b, 0, 0)),
        compiler_params=_params(),
        name="na_context",
    )(qkv_c, qkv_c, qkv_c)


def _residual_mlp(out_l, h0, m, gains, w1_ref, w2_ref, ff_chunk):
    h1 = h0 + m[2:3] * _rms(out_l, gains[1:2])
    hf = (_rms(h1, gains[2:3]) * (1.0 + m[4:5]) + m[3:4]).astype(BF16)
    d_ff = w1_ref.shape[1]
    acc = None
    for kf in range(d_ff // ff_chunk):
        cs = slice(kf * ff_chunk, (kf + 1) * ff_chunk)
        hid = jnp.maximum(jnp.dot(hf, w1_ref[:, cs], preferred_element_type=F32), 0.0)
        part = jnp.dot((hid * hid).astype(BF16), w2_ref[cs, :], preferred_element_type=F32)
        acc = part if acc is None else acc + part
    return h1 + m[5:6] * _rms(acc, gains[3:4])


def _post_even_kernel(yp_ref, na_ref, h_ref, mod_ref, g_ref, wglu_ref, wo_ref, w1_ref, w2_ref, o_ref, ys_ref,
                      *, ff_chunk):
    _pairs_to_tokens(yp_ref, ys_ref, ys_ref.shape[1] // S5_T)
    g = _gelu_tanh(jnp.concatenate([ys_ref[blk] for blk in range(ys_ref.shape[0])], axis=-1))
    gate = _sigmoid(jnp.dot(g.astype(BF16), wglu_ref[...], preferred_element_type=F32))
    s5 = (g * gate).astype(BF16)
    sw = s5.shape[1]
    out_l = (jnp.dot(s5, wo_ref[0:sw, :], preferred_element_type=F32)
             + jnp.dot(na_ref[0], wo_ref[sw:, :], preferred_element_type=F32))
    o_ref[0] = _residual_mlp(out_l, h_ref[0], mod_ref[0], g_ref[...], w1_ref, w2_ref, ff_chunk)


def _post_odd_kernel(mix_ref, h_ref, mod_ref, g_ref, wf_ref, w1_ref, w2_ref, o_ref, *, ff_chunk):
    out_l = jnp.dot(mix_ref[...], wf_ref[...], preferred_element_type=F32)
    o_ref[0] = _residual_mlp(out_l, h_ref[0], mod_ref[0], g_ref[...], w1_ref, w2_ref, ff_chunk)


def _post_even(y_pairs, na, h, mod, gains, w_glu, w_out, w1, w2, tm):
    bsz, seq, d = h.shape
    npair = y_pairs.shape[0]
    pw = S5_T * S5_PAIR
    tok = lambda w: pl.BlockSpec((1, tm, w), lambda b, i: (b, i, 0))
    return pl.pallas_call(
        functools.partial(_post_even_kernel, ff_chunk=1024),
        out_shape=jax.ShapeDtypeStruct(h.shape, F32),
        grid=(bsz, seq // tm),
        in_specs=[pl.BlockSpec((npair, tm // S5_T, pw), lambda b, i: (0, i, b)), tok(na.shape[-1]), tok(d),
                  pl.BlockSpec((1, 8, d), lambda b, i: (b, 0, 0)),
                  _resident(gains.shape), _resident(w_glu.shape), _resident(w_out.shape),
                  _resident(w1.shape), _resident(w2.shape)],
        out_specs=tok(d),
        scratch_shapes=[pltpu.VMEM((npair * S5_PAIR // LANES, tm, LANES), F32)],
        compiler_params=_params(),
        name="post_even",
    )(y_pairs, na, h, mod, gains, w_glu, w_out, w1, w2)


def _post_odd(mix, h, mod, gains, w_f, w1, w2, tm):
    bsz, seq, d = h.shape
    tok = pl.BlockSpec((1, tm, d), lambda b, i: (b, i, 0))
    return pl.pallas_call(
        functools.partial(_post_odd_kernel, ff_chunk=1024),
        out_shape=jax.ShapeDtypeStruct(h.shape, F32),
        grid=(bsz, seq // tm),
        in_specs=[pl.BlockSpec((tm, d), lambda b, i: (i, b)), tok,
                  pl.BlockSpec((1, 8, d), lambda b, i: (b, 0, 0)),
                  _resident(gains.shape), _resident(w_f.shape), _resident(w1.shape), _resident(w2.shape)],
        out_specs=tok,
        compiler_params=_params(),
        name="post_odd",
    )(mix, h, mod, gains, w_f, w1, w2)


def kernel(x, c, ctx, c_ctx, w_mod, b_mod, norm_g, w_in, w_out_even, s5_lam_re, s5_lam_im, s5_log_dt,
           s5_b_re, s5_b_im, s5_c_re, s5_c_im, s5_d, s5_w_glu, na_rpb, w_fourier, w_ff1, w_ff2):
    bsz, seq, d = x.shape
    lc = ctx.shape[1]
    depth = w_mod.shape[0]
    s5w = s5_d.shape[-1]
    last_ctx_layer = 2 * ((depth - 1) // 2)
    tm_lat, tm_ctx = 512, lc

    mod_rows = 16
    a = jnp.concatenate([c, c_ctx[None, :], jnp.zeros((mod_rows - bsz - 1, d), F32)], axis=0)
    mod = _modulation(a, w_mod, b_mod)
    pad_mod = lambda m: jnp.pad(m, ((0, 0), (0, 0), (0, 8 - N_MOD), (0, 0)))
    mod_l = pad_mod(mod[:, :bsz].reshape(depth, bsz, N_MOD, d))
    mod_c = pad_mod(jnp.broadcast_to(mod[:, bsz:bsz + 1].reshape(depth, 1, N_MOD, d), (depth, bsz, N_MOD, d)))
    gains = jnp.pad(norm_g.astype(F32), ((0, 0), (0, 4), (0, 0)))

    na_w = (w_in.shape[-1] - s5w) // 3
    qscale = jnp.concatenate([jnp.ones((s5w,), F32), jnp.full((na_w,), NA_HEAD_DIM ** -0.5, F32),
                              jnp.ones((2 * na_w,), F32)])
    w_in_b = (w_in * qscale).astype(BF16)
    w_out_b = w_out_even.astype(BF16)
    w_glu_b = s5_w_glu.astype(BF16)
    w_f_b = w_fourier.astype(BF16)
    w1_b = w_ff1.astype(BF16)
    w2_b = w_ff2.astype(BF16)

    wbig, wout, dec, dsk = _s5_prepare(s5_lam_re, s5_lam_im, s5_log_dt, s5_b_re, s5_b_im,
                                       s5_c_re, s5_c_im, s5_d)
    dft_d = _dft_pair(d, 32, right=True)
    dft_l = _dft_pair(seq, 64, right=False)
    dft_c = _dft_pair(lc, 16, right=False)

    h, s = x, ctx
    for layer in range(depth):
        need_ctx = layer <= last_ctx_layer
        upd_ctx = layer < last_ctx_layer
        g_l = gains[layer]
        if layer % 2 == 0:
            e = layer // 2
            u_l, qkv_l = _pre_even(h, mod_l[layer], g_l, w_in_b[e], tm_lat)
            u_c, qkv_c = _pre_even(s, mod_c[layer], g_l, w_in_b[e], tm_ctx)
            rows = lambda t: t.reshape(t.shape[0], t.shape[1] * bsz, t.shape[2] // bsz)
            y_c, y_l = _s5_chunked(rows(u_c), rows(u_l), wbig[e], wout[e], dec[e], dsk[e], bsz)
            y_c, y_l = y_c.reshape(u_c.shape), y_l.reshape(u_l.shape)
            bias = _na_bias_table(na_rpb[e], seq // GRID_W)
            na_l = _na_latent(qkv_l, qkv_c, bias)
            h_new = _post_even(y_l, na_l, h, mod_l[layer], g_l, w_glu_b[e], w_out_b[e],
                               w1_b[layer], w2_b[layer], tm_lat)
            if upd_ctx:
                na_c = _na_context(qkv_c)
                s = _post_even(y_c, na_c, s, mod_c[layer], g_l, w_glu_b[e], w_out_b[e],
                               w1_b[layer], w2_b[layer], tm_ctx)
            h = h_new
        else:
            o = layer // 2
            ab = _pre_odd(h, mod_l[layer], g_l, dft_d, tm_lat)
            mix = _matmul(dft_l, ab.reshape(2 * seq, bsz * d), 1024, 2048, 1024, BF16)
            h_new = _post_odd(mix, h, mod_l[layer], g_l, w_f_b[o], w1_b[layer], w2_b[layer], tm_lat)
            if upd_ctx:
                ab_c = _pre_odd(s, mod_c[layer], g_l, dft_d, tm_ctx)
                mix_c = _matmul(dft_c, ab_c.reshape(2 * lc, bsz * d), lc, 2048, 2 * lc, BF16)
                s = _post_odd(mix_c, s, mod_c[layer], g_l, w_f_b[o], w1_b[layer], w2_b[layer], tm_ctx)
            h = h_new
    return h
```

```python
import functools
import math

import jax
import jax.numpy as jnp
import numpy as np
from jax import lax
from jax.experimental import pallas as pl
from jax.experimental.pallas import tpu as pltpu

F32 = jnp.float32
BF16 = jnp.bfloat16
EPS = 1e-6
NEG_INF = -1e30

GRID_W = 64
S5_GROUP = 16
S5_STATE = 64
S5_T = 16
S5_PAIR = 2 * S5_GROUP
LANES = 128
TOK_TILE = 64
FF_CHUNK = 1024
NA_HEAD_DIM = 64
NA_WIN_R = 8
NA_WIN_C = 16
NA_QROWS = 4
NA_KROWS = 12
N_MOD = 6

V7X_VMEM_LIMIT_BYTES = 56 * 1024 * 1024
HIGHEST = lax.Precision.HIGHEST


def _params(**kw):
    return pltpu.CompilerParams(vmem_limit_bytes=V7X_VMEM_LIMIT_BYTES, **kw)


def _resident(shape):
    nd = len(shape)
    return pl.BlockSpec(shape, lambda *_: (0,) * nd, pipeline_mode=pl.Buffered(1))


def _layer_spec(stacked, idx):
    nd = stacked.ndim
    return pl.BlockSpec((None,) + stacked.shape[1:], lambda *_: (idx,) + (0,) * (nd - 1),
                        pipeline_mode=pl.Buffered(1))


def _rms(x, g):
    ms = jnp.mean(x * x, axis=-1, keepdims=True)
    return x * lax.rsqrt(ms + EPS) * g


def _sigmoid(x):
    return 1.0 / (1.0 + jnp.exp(-x))


def _gelu_tanh(x):
    c = math.sqrt(2.0 / math.pi)
    return x * (0.5 * (1.0 + jnp.tanh(c * (x + 0.044715 * (x * x * x)))))


def _mod_kernel(a_ref, w_ref, b_ref, o_ref):
    a = a_ref[...]
    act = a * _sigmoid(a)
    o_ref[0] = jnp.dot(act.astype(BF16), w_ref[0].astype(BF16), preferred_element_type=F32) + b_ref[0]


def _modulation(a, w_mod, b_mod):
    depth, d, n = w_mod.shape
    rows = a.shape[0]
    tn = 1536
    return pl.pallas_call(
        _mod_kernel,
        out_shape=jax.ShapeDtypeStruct((depth, rows, n), F32),
        grid=(depth, n // tn),
        in_specs=[
            pl.BlockSpec((rows, d), lambda l, j: (0, 0)),
            pl.BlockSpec((1, d, tn), lambda l, j: (l, 0, j)),
            pl.BlockSpec((1, 1, tn), lambda l, j: (l, 0, j)),
        ],
        out_specs=pl.BlockSpec((1, rows, tn), lambda l, j: (l, 0, j)),
        compiler_params=_params(),
        name="adaln_mod",
    )(a, w_mod, b_mod.reshape(depth, 1, n))


def _lane_quarter(shape):
    return lax.broadcasted_iota(jnp.int32, shape, 1) // S5_PAIR


def _tokens_to_pairs(tok_ref, pair_ref, bsz):
    per = LANES // S5_PAIR
    tt = tok_ref.shape[1] // bsz
    n_chunks = tt // S5_T
    quarter = _lane_quarter((n_chunks * bsz, LANES))
    for blk in range(tok_ref.shape[0]):
        for m in range(S5_T // per):
            src = [jnp.concatenate([tok_ref[blk, pl.ds(c * S5_T + per * m + jj, bsz, stride=tt), :]
                                    for c in range(n_chunks)], axis=0) for jj in range(per)]
            for qq in range(per):
                dest = None
                for jj in range(per):
                    shift = ((jj - qq) % per) * S5_PAIR
                    r = src[jj] if shift == 0 else pltpu.roll(src[jj], shift, axis=1)
                    dest = r if dest is None else jnp.where(quarter == jj, r, dest)
                pair_ref[per * blk + qq, :, m * LANES:(m + 1) * LANES] = dest


def _pairs_to_tokens(pair_ref, tok_ref, bsz):
    per = LANES // S5_PAIR
    tt = tok_ref.shape[1] // bsz
    n_chunks = tt // S5_T
    quarter = _lane_quarter((n_chunks * bsz, LANES))
    for blk in range(tok_ref.shape[0]):
        for m in range(S5_T // per):
            src = [pair_ref[per * blk + qq, :, m * LANES:(m + 1) * LANES] for qq in range(per)]
            for ts in range(per):
                dest = None
                for qq in range(per):
                    shift = ((qq - ts) % per) * S5_PAIR
                    r = src[qq] if shift == 0 else pltpu.roll(src[qq], shift, axis=1)
                    dest = r if dest is None else jnp.where(quarter == qq, r, dest)
                for c in range(n_chunks):
                    tok_ref[blk, pl.ds(c * S5_T + per * m + ts, bsz, stride=tt), :] = dest[c * bsz:(c + 1) * bsz]


def _modulated(h_ref, mod_ref, gain, shift_row, scale_row):
    parts = []
    for b in range(h_ref.shape[0]):
        m = mod_ref[b]
        y = _rms(h_ref[b], gain) * (1.0 + m[scale_row:scale_row + 1]) + m[shift_row:shift_row + 1]
        parts.append(y.astype(BF16))
    return jnp.concatenate(parts, axis=0)


def _pre_even_kernel(h_ref, mod_ref, g_ref, w_ref, up_ref, qkv_ref, us_ref, *, s5_width):
    bsz, tt, _ = h_ref.shape
    hl = _modulated(h_ref, mod_ref, g_ref[0:1], 0, 1)
    z = jnp.dot(hl, w_ref[...], preferred_element_type=F32)
    for blk in range(s5_width // LANES):
        us_ref[blk] = z[:, blk * LANES:(blk + 1) * LANES]
    for b in range(bsz):
        qkv_ref[b] = z[b * tt:(b + 1) * tt, s5_width:].astype(BF16)
    _tokens_to_pairs(us_ref, up_ref, bsz)


def _token_grid(seq):
    return (seq // TOK_TILE,)


def _tok_spec(bsz, width):
    return pl.BlockSpec((bsz, TOK_TILE, width), lambda i: (0, i, 0))


def _pre_even(h, mod, gains, w_in, layer, e):
    bsz, seq, d = h.shape
    n = w_in.shape[-1]
    s5w = d // 2
    npair = s5w // S5_PAIR
    pw = S5_T * S5_PAIR
    prow = TOK_TILE // S5_T * bsz
    return pl.pallas_call(
        functools.partial(_pre_even_kernel, s5_width=s5w),
        out_shape=(jax.ShapeDtypeStruct((npair, seq // S5_T * bsz, pw), F32),
                   jax.ShapeDtypeStruct((bsz, seq, n - s5w), BF16)),
        grid=_token_grid(seq),
        in_specs=[_tok_spec(bsz, d), _layer_spec(mod, layer), _layer_spec(gains, layer), _layer_spec(w_in, e)],
        out_specs=(pl.BlockSpec((npair, prow, pw), lambda i: (0, i, 0)), _tok_spec(bsz, n - s5w)),
        scratch_shapes=[pltpu.VMEM((s5w // LANES, bsz * TOK_TILE, LANES), F32)],
        compiler_params=_params(),
        name="pre_even",
    )(h, mod, gains, w_in)


def _pre_odd_kernel(h_ref, mod_ref, g_ref, w_ref, ab_ref, *, d):
    bsz, tt, _ = h_ref.shape
    hl = _modulated(h_ref, mod_ref, g_ref[0:1], 0, 1)
    z = jnp.dot(hl, w_ref[...], preferred_element_type=F32)
    for b in range(bsz):
        for s in range(2):
            ab_ref[s, :, b * d:(b + 1) * d] = z[b * tt:(b + 1) * tt, s * d:(s + 1) * d].astype(BF16)


def _pre_odd(h, mod, gains, w_cs, layer):
    bsz, seq, d = h.shape
    return pl.pallas_call(
        functools.partial(_pre_odd_kernel, d=d),
        out_shape=jax.ShapeDtypeStruct((2, seq, bsz * d), BF16),
        grid=_token_grid(seq),
        in_specs=[_tok_spec(bsz, d), _layer_spec(mod, layer), _layer_spec(gains, layer), _resident(w_cs.shape)],
        out_specs=pl.BlockSpec((2, TOK_TILE, bsz * d), lambda i: (0, i, 0)),
        compiler_params=_params(),
        name="pre_odd",
    )(h, mod, gains, w_cs)


def _matmul_kernel(a_ref, b_ref, o_ref, acc_ref, *, nk):
    k = pl.program_id(2)

    @pl.when(k == 0)
    def _():
        acc_ref[...] = jnp.zeros_like(acc_ref)

    acc_ref[...] += jnp.dot(a_ref[...], b_ref[...], preferred_element_type=F32)

    @pl.when(k == nk - 1)
    def _():
        o_ref[...] = acc_ref[...].astype(o_ref.dtype)


def _matmul(a, b, tm, tn, tk, out_dtype):
    m, kdim = a.shape
    n = b.shape[1]
    nk = kdim // tk
    return pl.pallas_call(
        functools.partial(_matmul_kernel, nk=nk),
        out_shape=jax.ShapeDtypeStruct((m, n), out_dtype),
        grid=(m // tm, n // tn, nk),
        in_specs=[pl.BlockSpec((tm, tk), lambda i, j, k: (i, k)),
                  pl.BlockSpec((tk, tn), lambda i, j, k: (k, j))],
        out_specs=pl.BlockSpec((tm, tn), lambda i, j, k: (i, j)),
        scratch_shapes=[pltpu.VMEM((tm, tn), F32)],
        compiler_params=_params(),
        name="fnet_seq_dft",
    )(a, b)


def _dft_pair(n, right):
    n0 = LANES
    n1 = n // n0
    k = jnp.arange(n, dtype=jnp.int32)[:, None]
    pa = (k * (jnp.arange(n1, dtype=jnp.int32)[None, :] * n0)) % n
    pb = (k * jnp.arange(n0, dtype=jnp.int32)[None, :]) % n
    w = 2.0 * math.pi / n
    scale = 1.0 / math.sqrt(n)
    sign = 1.0 if right else -1.0
    ca, sa = jnp.cos(pa.astype(F32) * w) * scale, jnp.sin(pa.astype(F32) * w) * scale
    cb, sb = jnp.cos(pb.astype(F32) * w), jnp.sin(pb.astype(F32) * w)
    a1 = jnp.stack([ca, sign * sa], axis=1)
    a2 = jnp.stack([sa, -sign * ca], axis=1)
    out = a1[..., None] * cb[:, None, None, :] - a2[..., None] * sb[:, None, None, :]
    return out.reshape(n, 2 * n).astype(BF16)


def _s5_prepare(lam_re, lam_im, log_dt, b_re, b_im, c_re, c_im, d_skip):
    t_len, hh, pp = S5_T, S5_GROUP, S5_STATE
    ne, _, gg, _ = lam_re.shape
    lam_re = jnp.minimum(lam_re.astype(F32), -1e-4)
    lam_im = lam_im.astype(F32)
    dt = jnp.exp(log_dt.astype(F32))[..., None]
    mag = jnp.exp(lam_re * dt)
    a_re = mag * jnp.cos(lam_im * dt)
    a_im = mag * jnp.sin(lam_im * dt)
    den = lam_re * lam_re + lam_im * lam_im
    num_re = a_re - 1.0
    f_re = (num_re * lam_re + a_im * lam_im) / den
    f_im = (a_im * lam_re - num_re * lam_im) / den
    b_re = b_re.astype(F32)
    b_im = b_im.astype(F32)
    bb_re = f_re[..., None] * b_re - f_im[..., None] * b_im
    bb_im = f_re[..., None] * b_im + f_im[..., None] * b_re
    tau = jnp.arange(t_len + 1, dtype=F32).reshape(-1, 1, 1, 1, 1)
    pmag = jnp.exp(lam_re[None] * dt[None] * tau)
    p_re = pmag * jnp.cos(lam_im[None] * dt[None] * tau)
    p_im = pmag * jnp.sin(lam_im[None] * dt[None] * tau)
    ab_re = p_re[..., None] * bb_re[None] - p_im[..., None] * bb_im[None]
    ab_im = p_re[..., None] * bb_im[None] + p_im[..., None] * bb_re[None]
    c_re = c_re.astype(F32)
    c_im = c_im.astype(F32)
    kk = (jnp.einsum('edgkp,tedgph->tedgkh', c_re, ab_re, precision=HIGHEST)
          - jnp.einsum('edgkp,tedgph->tedgkh', c_im, ab_im, precision=HIGHEST))
    jj = jnp.arange(t_len)[:, None]
    tt = jnp.arange(t_len)[None, :]
    diff = tt - jj
    kf = kk[:, :, 0][jnp.clip(diff, 0, t_len - 1)] * (diff >= 0).astype(F32)[:, :, None, None, None, None]
    kb = kk[:, :, 1][jnp.clip(-diff, 0, t_len - 1)] * (diff <= 0).astype(F32)[:, :, None, None, None, None]
    toep = kf + kb
    toep = jnp.transpose(toep, (2, 3, 0, 5, 1, 4)).reshape(ne, gg, t_len * hh, t_len * hh)
    rev = jnp.arange(t_len - 1, -1, -1)

    def st(x, d, idx):
        return jnp.transpose(x[idx, :, d], (1, 2, 0, 4, 3)).reshape(ne, gg, t_len * hh, pp)

    fwd_idx, bwd_idx = rev, jnp.arange(t_len)
    st_blocks = [st(ab_re, 0, fwd_idx), st(ab_im, 0, fwd_idx), st(ab_re, 1, bwd_idx), st(ab_im, 1, bwd_idx)]
    e_re = c_re[None] * p_re[:, :, :, :, None, :] - c_im[None] * p_im[:, :, :, :, None, :]
    e_im = c_re[None] * p_im[:, :, :, :, None, :] + c_im[None] * p_re[:, :, :, :, None, :]

    def ot(x, d, idx):
        return jnp.transpose(x[idx, :, d], (1, 2, 4, 0, 3)).reshape(ne, gg, pp, t_len * hh)

    f_tau = jnp.arange(1, t_len + 1)
    b_tau = jnp.arange(t_len, 0, -1)
    out_blocks = [ot(e_re, 0, f_tau), -ot(e_im, 0, f_tau), ot(e_re, 1, b_tau), -ot(e_im, 1, b_tau)]

    gp = gg // 2
    pw = t_len * 2 * hh
    eye = jnp.eye(2, dtype=F32)
    tp = jnp.transpose(toep.reshape(ne, gp, 2, t_len, hh, t_len, hh), (0, 1, 3, 2, 4, 5, 6))
    tp = tp[:, :, :, :, :, :, None, :] * eye[None, None, None, :, None, None, :, None]
    cols = [tp.reshape(ne, gp, pw, pw)]
    for x in st_blocks:
        sp = jnp.transpose(x.reshape(ne, gp, 2, t_len, hh, pp), (0, 1, 3, 2, 4, 5))
        sp = sp[:, :, :, :, :, None, :] * eye[None, None, None, :, None, :, None]
        cols.append(sp.reshape(ne, gp, pw, 2 * pp))
    wbig = jnp.concatenate(cols, axis=-1).astype(BF16)

    rows = []
    for x in out_blocks:
        op = x.reshape(ne, gp, 2, pp, t_len, hh)
        op = op[:, :, :, :, :, None, :] * eye[None, None, :, None, None, :, None]
        rows.append(op.reshape(ne, gp, 2 * pp, pw))
    wout = jnp.concatenate(rows, axis=2).astype(BF16)

    dec_rows = [p_re[t_len, :, 0], p_im[t_len, :, 0], p_re[t_len, :, 1], p_im[t_len, :, 1]]
    dec = jnp.stack([x.reshape(ne, gp, 2 * pp) for x in dec_rows], axis=2)
    dec = jnp.pad(dec, ((0, 0), (0, 0), (0, 4), (0, 0)))
    dsk = jnp.broadcast_to(d_skip.astype(F32).reshape(ne, gp, 1, 2 * hh), (ne, gp, t_len, 2 * hh))
    dsk = jnp.pad(dsk.reshape(ne, gp, 1, pw), ((0, 0), (0, 0), (0, 7), (0, 0)))
    return wbig, wout, dec, dsk


def _s5_kernel(uc_ref, ul_ref, wbig_ref, wout_ref, dec_ref, dsk_ref, yc_ref, yl_ref, z_ref, sin_ref, *, bsz):
    rows_c, rows_l = uc_ref.shape[1], ul_ref.shape[1]
    n_ctx, n_chunks = rows_c // bsz, (rows_c + rows_l) // bsz
    tr = 128
    wy = S5_T * S5_PAIR
    sw = 2 * S5_STATE
    parts = ((uc_ref, yc_ref, 0, rows_c), (ul_ref, yl_ref, rows_c, rows_l))

    for u_ref, _, base, nrows in parts:
        def stage1(i, carry, u_ref=u_ref, base=base):
            r = pl.multiple_of(i * tr, tr)
            ub = u_ref[0, pl.ds(r, tr), :].astype(BF16)
            z_ref[pl.ds(base + r, tr), :] = jnp.dot(ub, wbig_ref[0], preferred_element_type=F32)
            return carry

        lax.fori_loop(0, nrows // tr, stage1, 0)

    dec = dec_ref[0]
    zero = jnp.zeros((bsz, sw), F32)

    def make_step(d):
        a_re = dec[2 * d:2 * d + 1]
        a_im = dec[2 * d + 1:2 * d + 2]
        lc = wy + 2 * d * sw
        sc = 2 * d * sw

        def step(c, carry):
            s_re, s_im = carry
            r = pl.multiple_of(c * bsz, bsz)
            sin_ref[pl.ds(r, bsz), sc:sc + sw] = s_re
            sin_ref[pl.ds(r, bsz), sc + sw:sc + 2 * sw] = s_im
            l_re = z_ref[pl.ds(r, bsz), lc:lc + sw]
            l_im = z_ref[pl.ds(r, bsz), lc + sw:lc + 2 * sw]
            return (a_re * s_re - a_im * s_im + l_re, a_re * s_im + a_im * s_re + l_im)

        return step

    fwd, bwd = make_step(0), make_step(1)
    lax.fori_loop(0, n_chunks, fwd, (zero, zero))
    carry = lax.fori_loop(0, n_ctx, lambda k, cr: bwd(n_ctx - 1 - k, cr), (zero, zero))
    lax.fori_loop(0, n_chunks - n_ctx, lambda k, cr: bwd(n_chunks - 1 - k, cr), carry)

    dsk = dsk_ref[0][0:1]

    for u_ref, y_ref, base, nrows in parts:
        def stage3(i, carry, u_ref=u_ref, y_ref=y_ref, base=base):
            r = pl.multiple_of(i * tr, tr)
            sb = sin_ref[pl.ds(base + r, tr), :].astype(BF16)
            y = (dsk * u_ref[0, pl.ds(r, tr), :] + z_ref[pl.ds(base + r, tr), 0:wy]
                 + jnp.dot(sb, wout_ref[0], preferred_element_type=F32))
            y_ref[0, pl.ds(r, tr), :] = y
            return carry

        lax.fori_loop(0, nrows // tr, stage3, 0)


def _s5_chunked(u_ctx, u_lat, wbig, wout, dec, dsk, e, bsz):
    gp, rows_c, width = u_ctx.shape
    rows_l = u_lat.shape[1]
    zc = wbig.shape[-1]
    per_pair = lambda shape: pl.BlockSpec((1,) + shape[1:], lambda q: (q, 0, 0))
    of_layer = lambda w: pl.BlockSpec((None, 1) + w.shape[2:], lambda q: (e, q, 0, 0))
    return pl.pallas_call(
        functools.partial(_s5_kernel, bsz=bsz),
        out_shape=(jax.ShapeDtypeStruct(u_ctx.shape, F32), jax.ShapeDtypeStruct(u_lat.shape, F32)),
        grid=(gp,),
        in_specs=[per_pair(u_ctx.shape), per_pair(u_lat.shape), of_layer(wbig), of_layer(wout),
                  of_layer(dec), of_layer(dsk)],
        out_specs=(per_pair(u_ctx.shape), per_pair(u_lat.shape)),
        scratch_shapes=[pltpu.VMEM((rows_c + rows_l, zc), F32), pltpu.VMEM((rows_c + rows_l, width), F32)],
        compiler_params=_params(),
        name="s5_chunked",
    )(u_ctx, u_lat, wbig, wout, dec, dsk)


def _attend_pairs(q_ref, k_refs, v_refs, bias_ref, o_ref, n_biased):
    width = q_ref.shape[-1]
    kb = k_refs[0].shape[1]
    lane = lax.broadcasted_iota(jnp.int32, (1, 2 * NA_HEAD_DIM), 1)
    outs = []
    for hp in range(width // (2 * NA_HEAD_DIM)):
        cs = slice(hp * 2 * NA_HEAD_DIM, (hp + 1) * 2 * NA_HEAD_DIM)
        qp = q_ref[0, :, cs]
        ks = [r[0, :, cs] for r in k_refs]
        vs = [r[0, :, cs] for r in v_refs]
        res = []
        for e in range(2):
            sel = (lane // NA_HEAD_DIM) == e
            qe = jnp.where(sel, qp, jnp.zeros_like(qp))
            parts = []
            for i, kk in enumerate(ks):
                s = lax.dot_general(qe, kk, (((1,), (1,)), ((), ())), preferred_element_type=F32)
                if i < n_biased:
                    s = s + bias_ref[0, 2 * hp + e, :, i * kb:(i + 1) * kb]
                parts.append(s)
            m = parts[0].max(axis=-1, keepdims=True)
            for s in parts[1:]:
                m = jnp.maximum(m, s.max(axis=-1, keepdims=True))
            ps = [jnp.exp(s - m) for s in parts]
            den = ps[0].sum(axis=-1, keepdims=True)
            for p in ps[1:]:
                den = den + p.sum(axis=-1, keepdims=True)
            acc = jnp.dot(ps[0].astype(BF16), vs[0], preferred_element_type=F32)
            for p, vv in zip(ps[1:], vs[1:]):
                acc = acc + jnp.dot(p.astype(BF16), vv, preferred_element_type=F32)
            res.append(acc / den)
        outs.append(jnp.where(lane < NA_HEAD_DIM, res[0], res[1]))
    o_ref[0] = jnp.concatenate(outs, axis=-1).astype(o_ref.dtype)


def _na_lat_kernel(q_ref, k0, k1, k2, v0, v1, v2, kc, vc, bias_ref, o_ref):
    _attend_pairs(q_ref, (k0, k1, k2, kc), (v0, v1, v2, vc), bias_ref, o_ref, 3)


def _na_ctx_kernel(q_ref, kc, vc, o_ref):
    _attend_pairs(q_ref, (kc,), (vc,), None, o_ref, 0)


def _na_bias_table(rpb, rows):
    nh = rpb.shape[0]
    nblk = rows // NA_QROWS
    blocks = np.array([0, 1, nblk - 1])
    kb0 = np.clip(blocks - 1, 0, nblk - NA_KROWS // NA_QROWS) * NA_QROWS
    qr = blocks[:, None] * NA_QROWS + np.arange(NA_QROWS)[None, :]
    r0 = np.clip(qr - NA_WIN_R // 2, 0, rows - NA_WIN_R)
    kr = kb0[:, None] + np.arange(NA_KROWS)[None, :]
    row_ok = (kr[:, None, :] >= r0[:, :, None]) & (kr[:, None, :] < r0[:, :, None] + NA_WIN_R)
    n_dr = 2 * NA_WIN_R - 1
    dr = kr[:, None, :] - qr[:, :, None] + (NA_WIN_R - 1)
    cols = np.arange(GRID_W)
    c0 = np.clip(cols - NA_WIN_C // 2, 0, GRID_W - NA_WIN_C)
    col_ok = (cols[None, :] >= c0[:, None]) & (cols[None, :] < c0[:, None] + NA_WIN_C)
    dc = np.clip(cols[None, :] - cols[:, None], -(NA_WIN_C - 1), NA_WIN_C - 1) + (NA_WIN_C - 1)
    sel_c = (np.arange(2 * NA_WIN_C - 1)[:, None, None] == dc[None]).astype(np.float32)
    by_col = jnp.einsum('hrc,cqk->hrqk', rpb.astype(F32), sel_c, precision=HIGHEST)
    zero = jnp.zeros((nh, 1, GRID_W, GRID_W), F32)
    padded = jnp.concatenate([zero, by_col, zero], axis=1)
    two_rows = jnp.concatenate([padded[:, :-1], padded[:, 1:]], axis=-1)
    variants = []
    for v in range(len(blocks)):
        strips = []
        for i in range(NA_QROWS):
            first = np.clip(dr[v, i, 0::2], -1, n_dr - 1) + 1
            strips.append(jnp.concatenate([two_rows[:, int(s)] for s in first], axis=-1))
        variants.append(jnp.concatenate(strips, axis=1))
    bias = jnp.stack(variants, axis=0)
    ok = (row_ok[:, :, None, :, None] & col_ok[None, None, :, None, :]).reshape(
        len(blocks), 1, NA_QROWS * GRID_W, NA_KROWS * GRID_W)
    return jnp.where(ok, bias, NEG_INF)


def _na_latent(qkv, qkv_c, bias):
    bsz, seq, w3 = qkv.shape
    width = w3 // 3
    lc = qkv_c.shape[1]
    qb = NA_QROWS * GRID_W
    nblk = seq // qb
    nkb = NA_KROWS // NA_QROWS

    def kv_spec(col, s):
        return pl.BlockSpec((1, qb, width),
                            lambda b, a: (b, jnp.clip(a - 1, 0, nblk - nkb) + s, col))

    def variant(a):
        return (a > 0).astype(jnp.int32) + (a == nblk - 1).astype(jnp.int32)

    return pl.pallas_call(
        _na_lat_kernel,
        out_shape=jax.ShapeDtypeStruct((bsz, seq, width), BF16),
        grid=(bsz, nblk),
        in_specs=[pl.BlockSpec((1, qb, width), lambda b, a: (b, a, 0))]
        + [kv_spec(1, s) for s in range(nkb)] + [kv_spec(2, s) for s in range(nkb)]
        + [pl.BlockSpec((1, lc, width), lambda b, a: (b, 0, 1)),
           pl.BlockSpec((1, lc, width), lambda b, a: (b, 0, 2)),
           pl.BlockSpec((1,) + bias.shape[1:], lambda b, a: (variant(a), 0, 0, 0))],
        out_specs=pl.BlockSpec((1, qb, width), lambda b, a: (b, a, 0)),
        compiler_params=_params(),
        name="na_latent",
    )(qkv, *([qkv] * (2 * nkb)), qkv_c, qkv_c, bias)


def _na_context(qkv_c):
    bsz, lc, w3 = qkv_c.shape
    width = w3 // 3
    return pl.pallas_call(
        _na_ctx_kernel,
        out_shape=jax.ShapeDtypeStruct((bsz, lc, width), BF16),
        grid=(bsz,),
        in_specs=[pl.BlockSpec((1, lc, width), lambda b: (b, 0, 0)),
                  pl.BlockSpec((1, lc, width), lambda b: (b, 0, 1)),
                  pl.BlockSpec((1, lc, width), lambda b: (b, 0, 2))],
        out_specs=pl.BlockSpec((1, lc, width), lambda b: (b, 0, 0)),
        compiler_params=_params(),
        name="na_context",
    )(qkv_c, qkv_c, qkv_c)


def _residual_mlp(out_l, h_ref, mod_ref, gains, w1_ref, w2_ref, o_ref):
    bsz, tt, _ = h_ref.shape
    parts = []
    for b in range(bsz):
        m = mod_ref[b]
        h1 = h_ref[b] + m[2:3] * _rms(out_l[b * tt:(b + 1) * tt], gains[1:2])
        o_ref[b] = h1
        parts.append((_rms(h1, gains[2:3]) * (1.0 + m[4:5]) + m[3:4]).astype(BF16))
    hf = jnp.concatenate(parts, axis=0)
    d_ff = w1_ref.shape[1]
    acc = None
    for kf in range(d_ff // FF_CHUNK):
        cs = slice(kf * FF_CHUNK, (kf + 1) * FF_CHUNK)
        hid = jnp.maximum(jnp.dot(hf, w1_ref[:, cs], preferred_element_type=F32), 0.0)
        part = jnp.dot((hid * hid).astype(BF16), w2_ref[cs, :], preferred_element_type=F32)
        acc = part if acc is None else acc + part
    for b in range(bsz):
        o_ref[b] = o_ref[b] + mod_ref[b][5:6] * _rms(acc[b * tt:(b + 1) * tt], gains[3:4])


def _post_even_kernel(yp_ref, na_ref, h_ref, mod_ref, g_ref, wglu_ref, wo_ref, w1_ref, w2_ref, o_ref, ys_ref):
    bsz, tt, na_w = na_ref.shape
    _pairs_to_tokens(yp_ref, ys_ref, bsz)
    g = _gelu_tanh(jnp.concatenate([ys_ref[blk] for blk in range(ys_ref.shape[0])], axis=-1))
    gate = _sigmoid(jnp.dot(g.astype(BF16), wglu_ref[...], preferred_element_type=F32))
    s5 = (g * gate).astype(BF16)
    sw = s5.shape[1]
    out_l = (jnp.dot(s5, wo_ref[0:sw, :], preferred_element_type=F32)
             + jnp.dot(na_ref[...].reshape(bsz * tt, na_w), wo_ref[sw:, :], preferred_element_type=F32))
    _residual_mlp(out_l, h_ref, mod_ref, g_ref[...], w1_ref, w2_ref, o_ref)


def _post_odd_kernel(mix_ref, h_ref, mod_ref, g_ref, wf_ref, w1_ref, w2_ref, o_ref):
    bsz, _, d = h_ref.shape
    mix = jnp.concatenate([mix_ref[:, b * d:(b + 1) * d] for b in range(bsz)], axis=0)
    out_l = jnp.dot(mix, wf_ref[...], preferred_element_type=F32)
    _residual_mlp(out_l, h_ref, mod_ref, g_ref[...], w1_ref, w2_ref, o_ref)


def _post_even(y_pairs, na, h, mod, gains, w_glu, w_out, w1, w2, layer, e):
    bsz, seq, d = h.shape
    npair = y_pairs.shape[0]
    pw = S5_T * S5_PAIR
    prow = TOK_TILE // S5_T * bsz
    return pl.pallas_call(
        _post_even_kernel,
        out_shape=jax.ShapeDtypeStruct(h.shape, F32),
        grid=_token_grid(seq),
        in_specs=[pl.BlockSpec((npair, prow, pw), lambda i: (0, i, 0)), _tok_spec(bsz, na.shape[-1]),
                  _tok_spec(bsz, d), _layer_spec(mod, layer), _layer_spec(gains, layer),
                  _layer_spec(w_glu, e), _layer_spec(w_out, e), _layer_spec(w1, layer), _layer_spec(w2, layer)],
        out_specs=_tok_spec(bsz, d),
        scratch_shapes=[pltpu.VMEM((npair * S5_PAIR // LANES, bsz * TOK_TILE, LANES), F32)],
        compiler_params=_params(),
        name="post_even",
    )(y_pairs, na, h, mod, gains, w_glu, w_out, w1, w2)


def _post_odd(mix, h, mod, gains, w_f, w1, w2, layer, o):
    bsz, seq, d = h.shape
    return pl.pallas_call(
        _post_odd_kernel,
        out_shape=jax.ShapeDtypeStruct(h.shape, F32),
        grid=_token_grid(seq),
        in_specs=[pl.BlockSpec((TOK_TILE, bsz * d), lambda i: (i, 0)), _tok_spec(bsz, d),
                  _layer_spec(mod, layer), _layer_spec(gains, layer), _layer_spec(w_f, o),
                  _layer_spec(w1, layer), _layer_spec(w2, layer)],
        out_specs=_tok_spec(bsz, d),
        compiler_params=_params(),
        name="post_odd",
    )(mix, h, mod, gains, w_f, w1, w2)


def kernel(x, c, ctx, c_ctx, w_mod, b_mod, norm_g, w_in, w_out_even, s5_lam_re, s5_lam_im, s5_log_dt,
           s5_b_re, s5_b_im, s5_c_re, s5_c_im, s5_d, s5_w_glu, na_rpb, w_fourier, w_ff1, w_ff2):
    bsz, seq, d = x.shape
    lc = ctx.shape[1]
    depth = w_mod.shape[0]
    s5w = s5_d.shape[-1]
    last_ctx_layer = 2 * ((depth - 1) // 2)

    mod_rows = 16
    a = jnp.concatenate([c, c_ctx[None, :], jnp.zeros((mod_rows - bsz - 1, d), F32)], axis=0)
    mod = _modulation(a, w_mod, b_mod)
    pad_mod = lambda m: jnp.pad(m, ((0, 0), (0, 0), (0, 8 - N_MOD), (0, 0)))
    mod_l = pad_mod(mod[:, :bsz].reshape(depth, bsz, N_MOD, d))
    mod_c = pad_mod(jnp.broadcast_to(mod[:, bsz:bsz + 1].reshape(depth, 1, N_MOD, d), (depth, bsz, N_MOD, d)))
    gains = jnp.pad(norm_g.astype(F32), ((0, 0), (0, 4), (0, 0)))

    na_w = (w_in.shape[-1] - s5w) // 3
    qscale = jnp.concatenate([jnp.ones((s5w,), F32), jnp.full((na_w,), NA_HEAD_DIM ** -0.5, F32),
                              jnp.ones((2 * na_w,), F32)])
    w_in_b = (w_in * qscale).astype(BF16)
    w_out_b = w_out_even.astype(BF16)
    w_glu_b = s5_w_glu.astype(BF16)
    w_f_b = w_fourier.astype(BF16)
    w1_b = w_ff1.astype(BF16)
    w2_b = w_ff2.astype(BF16)

    wbig, wout, dec, dsk = _s5_prepare(s5_lam_re, s5_lam_im, s5_log_dt, s5_b_re, s5_b_im,
                                       s5_c_re, s5_c_im, s5_d)
    dft_d = _dft_pair(d, right=True)
    dft_l = _dft_pair(seq, right=False)
    dft_c = _dft_pair(lc, right=False)

    h, s = x, ctx
    for layer in range(depth):
        upd_ctx = layer < last_ctx_layer
        if layer % 2 == 0:
            e = layer // 2
            u_l, qkv_l = _pre_even(h, mod_l, gains, w_in_b, layer, e)
            u_c, qkv_c = _pre_even(s, mod_c, gains, w_in_b, layer, e)
            y_c, y_l = _s5_chunked(u_c, u_l, wbig, wout, dec, dsk, e, bsz)
            bias = _na_bias_table(na_rpb[e], seq // GRID_W)
            na_l = _na_latent(qkv_l, qkv_c, bias)
            h_new = _post_even(y_l, na_l, h, mod_l, gains, w_glu_b, w_out_b, w1_b, w2_b, layer, e)
            if upd_ctx:
                na_c = _na_context(qkv_c)
                s = _post_even(y_c, na_c, s, mod_c, gains, w_glu_b, w_out_b, w1_b, w2_b, layer, e)
            h = h_new
        else:
            o = layer // 2
            ab = _pre_odd(h, mod_l, gains, dft_d, layer)
            mix = _matmul(dft_l, ab.reshape(2 * seq, bsz * d), 1024, 2048, 1024, BF16)
            h_new = _post_odd(mix, h, mod_l, gains, w_f_b, w1_b, w2_b, layer, o)
            if upd_ctx:
                ab_c = _pre_odd(s, mod_c, gains, dft_d, layer)
                mix_c = _matmul(dft_c, ab_c.reshape(2 * lc, bsz * d), lc, 2048, 2 * lc, BF16)
                s = _post_odd(mix_c, s, mod_c, gains, w_f_b, w1_b, w2_b, layer, o)
            h = h_new
    return h
```

```python
import functools
import math

import jax
import jax.numpy as jnp
import numpy as np
from jax import lax
from jax.experimental import pallas as pl
from jax.experimental.pallas import tpu as pltpu

F32 = jnp.float32
BF16 = jnp.bfloat16
EPS = 1e-6
NEG_INF = -1e30

GRID_W = 64
S5_GROUP = 16
S5_STATE = 64
S5_T = 16
S5_PAIR = 2 * S5_GROUP
LANES = 128
TOK_TILE = 64
FF_CHUNK = 1024
NA_HEAD_DIM = 64
NA_WIN_R = 8
NA_WIN_C = 16
NA_QROWS = 4
NA_KROWS = 12
N_MOD = 6

V7X_VMEM_LIMIT_BYTES = 56 * 1024 * 1024
HIGHEST = lax.Precision.HIGHEST


def _params(**kw):
    return pltpu.CompilerParams(vmem_limit_bytes=V7X_VMEM_LIMIT_BYTES, **kw)


def _resident(shape):
    nd = len(shape)
    return pl.BlockSpec(shape, lambda *_: (0,) * nd, pipeline_mode=pl.Buffered(1))


def _layer_spec(stacked, idx):
    nd = stacked.ndim
    return pl.BlockSpec((None,) + stacked.shape[1:], lambda *_: (idx,) + (0,) * (nd - 1),
                        pipeline_mode=pl.Buffered(1))


def _rms(x, g):
    ms = jnp.mean(x * x, axis=-1, keepdims=True)
    return x * lax.rsqrt(ms + EPS) * g


def _sigmoid(x):
    return 1.0 / (1.0 + jnp.exp(-x))


def _gelu_tanh(x):
    c = math.sqrt(2.0 / math.pi)
    return x * (0.5 * (1.0 + jnp.tanh(c * (x + 0.044715 * (x * x * x)))))


def _mod_kernel(a_ref, w_ref, b_ref, o_ref):
    a = a_ref[...]
    act = a * _sigmoid(a)
    o_ref[0] = jnp.dot(act.astype(BF16), w_ref[0].astype(BF16), preferred_element_type=F32) + b_ref[0]


def _modulation(a, w_mod, b_mod):
    depth, d, n = w_mod.shape
    rows = a.shape[0]
    tn = 1536
    return pl.pallas_call(
        _mod_kernel,
        out_shape=jax.ShapeDtypeStruct((depth, rows, n), F32),
        grid=(depth, n // tn),
        in_specs=[
            pl.BlockSpec((rows, d), lambda l, j: (0, 0)),
            pl.BlockSpec((1, d, tn), lambda l, j: (l, 0, j)),
            pl.BlockSpec((1, 1, tn), lambda l, j: (l, 0, j)),
        ],
        out_specs=pl.BlockSpec((1, rows, tn), lambda l, j: (l, 0, j)),
        compiler_params=_params(),
        name="adaln_mod",
    )(a, w_mod, b_mod.reshape(depth, 1, n))


def _lane_quarter(shape):
    return lax.broadcasted_iota(jnp.int32, shape, 1) // S5_PAIR


def _tokens_to_pairs(tok_ref, pair_ref, bsz):
    per = LANES // S5_PAIR
    tt = tok_ref.shape[1] // bsz
    n_chunks = tt // S5_T
    quarter = _lane_quarter((n_chunks * bsz, LANES))
    for blk in range(tok_ref.shape[0]):
        for m in range(S5_T // per):
            src = [jnp.concatenate([tok_ref[blk, pl.ds(c * S5_T + per * m + jj, bsz, stride=tt), :]
                                    for c in range(n_chunks)], axis=0) for jj in range(per)]
            for qq in range(per):
                dest = None
                for jj in range(per):
                    shift = ((jj - qq) % per) * S5_PAIR
                    r = src[jj] if shift == 0 else pltpu.roll(src[jj], shift, axis=1)
                    dest = r if dest is None else jnp.where(quarter == jj, r, dest)
                pair_ref[per * blk + qq, :, m * LANES:(m + 1) * LANES] = dest


def _pairs_to_tokens(pair_ref, tok_ref, bsz):
    per = LANES // S5_PAIR
    tt = tok_ref.shape[1] // bsz
    n_chunks = tt // S5_T
    quarter = _lane_quarter((n_chunks * bsz, LANES))
    for blk in range(tok_ref.shape[0]):
        for m in range(S5_T // per):
            src = [pair_ref[per * blk + qq, :, m * LANES:(m + 1) * LANES] for qq in range(per)]
            for ts in range(per):
                dest = None
                for qq in range(per):
                    shift = ((qq - ts) % per) * S5_PAIR
                    r = src[qq] if shift == 0 else pltpu.roll(src[qq], shift, axis=1)
                    dest = r if dest is None else jnp.where(quarter == qq, r, dest)
                for c in range(n_chunks):
                    tok_ref[blk, pl.ds(c * S5_T + per * m + ts, bsz, stride=tt), :] = dest[c * bsz:(c + 1) * bsz]


def _modulated(h_ref, mod_ref, gain, shift_row, scale_row):
    parts = []
    for b in range(h_ref.shape[0]):
        m = mod_ref[b]
        y = _rms(h_ref[b], gain) * (1.0 + m[scale_row:scale_row + 1]) + m[shift_row:shift_row + 1]
        parts.append(y.astype(BF16))
    return jnp.concatenate(parts, axis=0)


def _pre_even_kernel(h_ref, mod_ref, g_ref, w_ref, up_ref, qkv_ref, us_ref, *, s5_width):
    bsz, tt, _ = h_ref.shape
    hl = _modulated(h_ref, mod_ref, g_ref[0:1], 0, 1)
    z = jnp.dot(hl, w_ref[...], preferred_element_type=F32)
    for blk in range(s5_width // LANES):
        us_ref[blk] = z[:, blk * LANES:(blk + 1) * LANES]
    for b in range(bsz):
        qkv_ref[b] = z[b * tt:(b + 1) * tt, s5_width:].astype(BF16)
    _tokens_to_pairs(us_ref, up_ref, bsz)


def _token_grid(seq):
    return (seq // TOK_TILE,)


def _tok_spec(bsz, width):
    return pl.BlockSpec((bsz, TOK_TILE, width), lambda i: (0, i, 0))


def _pre_even(h, mod, gains, w_in, layer, e):
    bsz, seq, d = h.shape
    n = w_in.shape[-1]
    s5w = d // 2
    npair = s5w // S5_PAIR
    pw = S5_T * S5_PAIR
    prow = TOK_TILE // S5_T * bsz
    return pl.pallas_call(
        functools.partial(_pre_even_kernel, s5_width=s5w),
        out_shape=(jax.ShapeDtypeStruct((npair, seq // S5_T * bsz, pw), F32),
                   jax.ShapeDtypeStruct((bsz, seq, n - s5w), BF16)),
        grid=_token_grid(seq),
        in_specs=[_tok_spec(bsz, d), _layer_spec(mod, layer), _layer_spec(gains, layer), _layer_spec(w_in, e)],
        out_specs=(pl.BlockSpec((npair, prow, pw), lambda i: (0, i, 0)), _tok_spec(bsz, n - s5w)),
        scratch_shapes=[pltpu.VMEM((s5w // LANES, bsz * TOK_TILE, LANES), F32)],
        compiler_params=_params(),
        name="pre_even",
    )(h, mod, gains, w_in)


def _pre_odd_kernel(h_ref, mod_ref, g_ref, w_ref, ab_ref, *, d):
    bsz, tt, _ = h_ref.shape
    hl = _modulated(h_ref, mod_ref, g_ref[0:1], 0, 1)
    z = jnp.dot(hl, w_ref[...], preferred_element_type=F32)
    for b in range(bsz):
        for s in range(2):
            ab_ref[s, :, b * d:(b + 1) * d] = z[b * tt:(b + 1) * tt, s * d:(s + 1) * d].astype(BF16)


def _pre_odd(h, mod, gains, w_cs, layer):
    bsz, seq, d = h.shape
    return pl.pallas_call(
        functools.partial(_pre_odd_kernel, d=d),
        out_shape=jax.ShapeDtypeStruct((2, seq, bsz * d), BF16),
        grid=_token_grid(seq),
        in_specs=[_tok_spec(bsz, d), _layer_spec(mod, layer), _layer_spec(gains, layer), _resident(w_cs.shape)],
        out_specs=pl.BlockSpec((2, TOK_TILE, bsz * d), lambda i: (0, i, 0)),
        compiler_params=_params(),
        name="pre_odd",
    )(h, mod, gains, w_cs)


def _matmul_kernel(a_ref, b_ref, o_ref, acc_ref, *, nk):
    k = pl.program_id(2)

    @pl.when(k == 0)
    def _():
        acc_ref[...] = jnp.zeros_like(acc_ref)

    acc_ref[...] += jnp.dot(a_ref[...], b_ref[...], preferred_element_type=F32)

    @pl.when(k == nk - 1)
    def _():
        o_ref[...] = acc_ref[...].astype(o_ref.dtype)


def _matmul(a, b, tm, tn, tk, out_dtype):
    m, kdim = a.shape
    n = b.shape[1]
    nk = kdim // tk
    return pl.pallas_call(
        functools.partial(_matmul_kernel, nk=nk),
        out_shape=jax.ShapeDtypeStruct((m, n), out_dtype),
        grid=(m // tm, n // tn, nk),
        in_specs=[pl.BlockSpec((tm, tk), lambda i, j, k: (i, k)),
                  pl.BlockSpec((tk, tn), lambda i, j, k: (k, j))],
        out_specs=pl.BlockSpec((tm, tn), lambda i, j, k: (i, j)),
        scratch_shapes=[pltpu.VMEM((tm, tn), F32)],
        compiler_params=_params(),
        name="fnet_seq_dft",
    )(a, b)


def _dft_pair(n, right):
    n0 = LANES
    n1 = n // n0
    k = jnp.arange(n, dtype=jnp.int32)[:, None]
    pa = (k * (jnp.arange(n1, dtype=jnp.int32)[None, :] * n0)) % n
    pb = (k * jnp.arange(n0, dtype=jnp.int32)[None, :]) % n
    w = 2.0 * math.pi / n
    scale = 1.0 / math.sqrt(n)
    sign = 1.0 if right else -1.0
    ca, sa = jnp.cos(pa.astype(F32) * w) * scale, jnp.sin(pa.astype(F32) * w) * scale
    cb, sb = jnp.cos(pb.astype(F32) * w), jnp.sin(pb.astype(F32) * w)
    a1 = jnp.concatenate([ca, sign * sa], axis=1)
    a2 = jnp.concatenate([sa, -sign * ca], axis=1)
    tr = min(n, 256)
    rows = lambda w_: pl.BlockSpec((tr, w_), lambda i: (i, 0))
    return pl.pallas_call(
        _dft_expand_kernel,
        out_shape=jax.ShapeDtypeStruct((n, 2 * n), BF16),
        grid=(n // tr,),
        in_specs=[rows(2 * n1), rows(2 * n1), rows(n0), rows(n0)],
        out_specs=rows(2 * n),
        compiler_params=_params(),
        name="dft_expand",
    )(a1, a2, cb, sb)


def _dft_expand_kernel(a1_ref, a2_ref, cb_ref, sb_ref, o_ref):
    cb, sb = cb_ref[...], sb_ref[...]
    for j in range(a1_ref.shape[1]):
        blk = a1_ref[:, j:j + 1] * cb - a2_ref[:, j:j + 1] * sb
        o_ref[:, j * LANES:(j + 1) * LANES] = blk.astype(BF16)


def _s5_prepare(lam_re, lam_im, log_dt, b_re, b_im, c_re, c_im, d_skip):
    t_len, hh, pp = S5_T, S5_GROUP, S5_STATE
    ne, _, gg, _ = lam_re.shape
    lam_re = jnp.minimum(lam_re.astype(F32), -1e-4)
    lam_im = lam_im.astype(F32)
    dt = jnp.exp(log_dt.astype(F32))[..., None]
    mag = jnp.exp(lam_re * dt)
    a_re = mag * jnp.cos(lam_im * dt)
    a_im = mag * jnp.sin(lam_im * dt)
    den = lam_re * lam_re + lam_im * lam_im
    num_re = a_re - 1.0
    f_re = (num_re * lam_re + a_im * lam_im) / den
    f_im = (a_im * lam_re - num_re * lam_im) / den
    b_re = b_re.astype(F32)
    b_im = b_im.astype(F32)
    bb_re = f_re[..., None] * b_re - f_im[..., None] * b_im
    bb_im = f_re[..., None] * b_im + f_im[..., None] * b_re
    tau = jnp.arange(t_len + 1, dtype=F32).reshape(1, 1, 1, -1, 1)
    lr, li, dtt = lam_re[..., None, :], lam_im[..., None, :], dt[..., None]
    pmag = jnp.exp(lr * dtt * tau)
    p_re = pmag * jnp.cos(li * dtt * tau)
    p_im = pmag * jnp.sin(li * dtt * tau)
    bbt_re = jnp.swapaxes(bb_re, -1, -2)[:, :, :, None]
    bbt_im = jnp.swapaxes(bb_im, -1, -2)[:, :, :, None]
    abt_re = p_re[..., None, :] * bbt_re - p_im[..., None, :] * bbt_im
    abt_im = p_re[..., None, :] * bbt_im + p_im[..., None, :] * bbt_re
    c_re = c_re.astype(F32)[:, :, :, None]
    c_im = c_im.astype(F32)[:, :, :, None]
    e_re = c_re * p_re[..., None, :] - c_im * p_im[..., None, :]
    e_im = c_re * p_im[..., None, :] + c_im * p_re[..., None, :]

    gp = gg // 2
    pw = t_len * 2 * hh

    def cat(re, im):
        tn = re.shape[2]
        re = re.reshape(ne, gp, 2, tn, hh, pp)
        im = im.reshape(ne, gp, 2, tn, hh, pp)
        z = jnp.zeros((ne, gp, tn, hh, pp), F32)
        e0 = jnp.concatenate([re[:, :, 0], z, im[:, :, 0], z], axis=-1)
        e1 = jnp.concatenate([z, re[:, :, 1], z, im[:, :, 1]], axis=-1)
        return jnp.stack([e0, e1], axis=3).reshape(ne, gp, tn * 2 * hh, 4 * pp)

    fwd_t = slice(0, t_len)
    rev_t = slice(t_len - 1, None, -1)
    st_cols, out_rows, resp = [], [], []
    for d in range(2):
        tau_k = fwd_t if d == 0 else rev_t
        bcat0 = cat(abt_re[:, d, :, 0:1], abt_im[:, d, :, 0:1])
        ecat = cat(e_re[:, d, :, tau_k], -e_im[:, d, :, tau_k])
        resp.append(jnp.einsum('xqrk,xqck->xqrc', bcat0, ecat, precision=HIGHEST))
        tau_s = rev_t if d == 0 else fwd_t
        st_cols.append(cat(abt_re[:, d, :, tau_s], abt_im[:, d, :, tau_s]))
        tau_o = slice(1, t_len + 1) if d == 0 else slice(t_len, 0, -1)
        out_rows.append(jnp.swapaxes(cat(e_re[:, d, :, tau_o], -e_im[:, d, :, tau_o]), -1, -2))

    span = pw - 2 * hh
    kf = jnp.pad(resp[0], ((0, 0), (0, 0), (0, 0), (span, 0)))
    kb = jnp.pad(resp[1], ((0, 0), (0, 0), (0, 0), (0, span)))
    blocks = []
    for j in range(t_len):
        lo_f = span - 2 * hh * j
        lo_b = 2 * hh * (t_len - 1 - j)
        blocks.append(kf[..., lo_f:lo_f + pw] + kb[..., lo_b:lo_b + pw])
    toep = jnp.stack(blocks, axis=2).reshape(ne, gp, pw, pw)
    wbig = jnp.concatenate([toep] + st_cols, axis=-1).astype(BF16)
    wout = jnp.concatenate(out_rows, axis=2).astype(BF16)

    dec_rows = [p_re[:, 0, :, t_len], p_im[:, 0, :, t_len], p_re[:, 1, :, t_len], p_im[:, 1, :, t_len]]
    dec = jnp.stack([x.reshape(ne, gp, 2 * pp) for x in dec_rows], axis=2)
    dec = jnp.pad(dec, ((0, 0), (0, 0), (0, 4), (0, 0)))
    dsk = jnp.broadcast_to(d_skip.astype(F32).reshape(ne, gp, 1, 2 * hh), (ne, gp, t_len, 2 * hh))
    dsk = jnp.pad(dsk.reshape(ne, gp, 1, pw), ((0, 0), (0, 0), (0, 7), (0, 0)))
    return wbig, wout, dec, dsk


def _s5_kernel(uc_ref, ul_ref, wbig_ref, wout_ref, dec_ref, dsk_ref, yc_ref, yl_ref, z_ref, sin_ref, *, bsz):
    rows_c, rows_l = uc_ref.shape[1], ul_ref.shape[1]
    n_ctx, n_chunks = rows_c // bsz, (rows_c + rows_l) // bsz
    tr = 128
    wy = S5_T * S5_PAIR
    sw = 2 * S5_STATE
    parts = ((uc_ref, yc_ref, 0, rows_c), (ul_ref, yl_ref, rows_c, rows_l))

    for u_ref, _, base, nrows in parts:
        def stage1(i, carry, u_ref=u_ref, base=base):
            r = pl.multiple_of(i * tr, tr)
            ub = u_ref[0, pl.ds(r, tr), :].astype(BF16)
            z_ref[pl.ds(base + r, tr), :] = jnp.dot(ub, wbig_ref[0], preferred_element_type=F32)
            return carry

        lax.fori_loop(0, nrows // tr, stage1, 0)

    dec = dec_ref[0]
    zero = jnp.zeros((bsz, sw), F32)

    def make_step(d):
        a_re = dec[2 * d:2 * d + 1]
        a_im = dec[2 * d + 1:2 * d + 2]
        lc = wy + 2 * d * sw
        sc = 2 * d * sw

        def step(c, carry):
            s_re, s_im = carry
            r = pl.multiple_of(c * bsz, bsz)
            sin_ref[pl.ds(r, bsz), sc:sc + sw] = s_re
            sin_ref[pl.ds(r, bsz), sc + sw:sc + 2 * sw] = s_im
            l_re = z_ref[pl.ds(r, bsz), lc:lc + sw]
            l_im = z_ref[pl.ds(r, bsz), lc + sw:lc + 2 * sw]
            return (a_re * s_re - a_im * s_im + l_re, a_re * s_im + a_im * s_re + l_im)

        return step

    fwd, bwd = make_step(0), make_step(1)
    lax.fori_loop(0, n_chunks, fwd, (zero, zero))
    carry = lax.fori_loop(0, n_ctx, lambda k, cr: bwd(n_ctx - 1 - k, cr), (zero, zero))
    lax.fori_loop(0, n_chunks - n_ctx, lambda k, cr: bwd(n_chunks - 1 - k, cr), carry)

    dsk = dsk_ref[0][0:1]

    for u_ref, y_ref, base, nrows in parts:
        def stage3(i, carry, u_ref=u_ref, y_ref=y_ref, base=base):
            r = pl.multiple_of(i * tr, tr)
            sb = sin_ref[pl.ds(base + r, tr), :].astype(BF16)
            y = (dsk * u_ref[0, pl.ds(r, tr), :] + z_ref[pl.ds(base + r, tr), 0:wy]
                 + jnp.dot(sb, wout_ref[0], preferred_element_type=F32))
            y_ref[0, pl.ds(r, tr), :] = y
            return carry

        lax.fori_loop(0, nrows // tr, stage3, 0)


def _s5_chunked(u_ctx, u_lat, wbig, wout, dec, dsk, e, bsz):
    gp, rows_c, width = u_ctx.shape
    rows_l = u_lat.shape[1]
    zc = wbig.shape[-1]
    per_pair = lambda shape: pl.BlockSpec((1,) + shape[1:], lambda q: (q, 0, 0))
    of_layer = lambda w: pl.BlockSpec((None, 1) + w.shape[2:], lambda q: (e, q, 0, 0))
    return pl.pallas_call(
        functools.partial(_s5_kernel, bsz=bsz),
        out_shape=(jax.ShapeDtypeStruct(u_ctx.shape, F32), jax.ShapeDtypeStruct(u_lat.shape, F32)),
        grid=(gp,),
        in_specs=[per_pair(u_ctx.shape), per_pair(u_lat.shape), of_layer(wbig), of_layer(wout),
                  of_layer(dec), of_layer(dsk)],
        out_specs=(per_pair(u_ctx.shape), per_pair(u_lat.shape)),
        scratch_shapes=[pltpu.VMEM((rows_c + rows_l, zc), F32), pltpu.VMEM((rows_c + rows_l, width), F32)],
        compiler_params=_params(),
        name="s5_chunked",
    )(u_ctx, u_lat, wbig, wout, dec, dsk)


def _attend_pairs(q_ref, k_refs, v_refs, bias_ref, o_ref, n_biased):
    width = q_ref.shape[-1]
    kb = k_refs[0].shape[1]
    lane = lax.broadcasted_iota(jnp.int32, (1, 2 * NA_HEAD_DIM), 1)
    outs = []
    for hp in range(width // (2 * NA_HEAD_DIM)):
        cs = slice(hp * 2 * NA_HEAD_DIM, (hp + 1) * 2 * NA_HEAD_DIM)
        qp = q_ref[0, :, cs]
        ks = [r[0, :, cs] for r in k_refs]
        vs = [r[0, :, cs] for r in v_refs]
        res = []
        for e in range(2):
            sel = (lane // NA_HEAD_DIM) == e
            qe = jnp.where(sel, qp, jnp.zeros_like(qp))
            parts = []
            for i, kk in enumerate(ks):
                s = lax.dot_general(qe, kk, (((1,), (1,)), ((), ())), preferred_element_type=F32)
                if i < n_biased:
                    s = s + bias_ref[0, 2 * hp + e, :, i * kb:(i + 1) * kb]
                parts.append(s)
            m = parts[0].max(axis=-1, keepdims=True)
            for s in parts[1:]:
                m = jnp.maximum(m, s.max(axis=-1, keepdims=True))
            ps = [jnp.exp(s - m) for s in parts]
            den = ps[0].sum(axis=-1, keepdims=True)
            for p in ps[1:]:
                den = den + p.sum(axis=-1, keepdims=True)
            acc = jnp.dot(ps[0].astype(BF16), vs[0], preferred_element_type=F32)
            for p, vv in zip(ps[1:], vs[1:]):
                acc = acc + jnp.dot(p.astype(BF16), vv, preferred_element_type=F32)
            res.append(acc / den)
        outs.append(jnp.where(lane < NA_HEAD_DIM, res[0], res[1]))
    o_ref[0] = jnp.concatenate(outs, axis=-1).astype(o_ref.dtype)


def _na_lat_kernel(q_ref, k0, k1, k2, v0, v1, v2, kc, vc, bias_ref, o_ref):
    _attend_pairs(q_ref, (k0, k1, k2, kc), (v0, v1, v2, vc), bias_ref, o_ref, 3)


def _na_ctx_kernel(q_ref, kc, vc, o_ref):
    _attend_pairs(q_ref, (kc,), (vc,), None, o_ref, 0)


def _na_bias_table(rpb, rows):
    nh = rpb.shape[0]
    nblk = rows // NA_QROWS
    blocks = np.array([0, 1, nblk - 1])
    kb0 = np.clip(blocks - 1, 0, nblk - NA_KROWS // NA_QROWS) * NA_QROWS
    qr = blocks[:, None] * NA_QROWS + np.arange(NA_QROWS)[None, :]
    r0 = np.clip(qr - NA_WIN_R // 2, 0, rows - NA_WIN_R)
    kr = kb0[:, None] + np.arange(NA_KROWS)[None, :]
    row_ok = (kr[:, None, :] >= r0[:, :, None]) & (kr[:, None, :] < r0[:, :, None] + NA_WIN_R)
    n_dr = 2 * NA_WIN_R - 1
    dr = kr[:, None, :] - qr[:, :, None] + (NA_WIN_R - 1)
    cols = np.arange(GRID_W)
    c0 = np.clip(cols - NA_WIN_C // 2, 0, GRID_W - NA_WIN_C)
    col_ok = (cols[None, :] >= c0[:, None]) & (cols[None, :] < c0[:, None] + NA_WIN_C)
    dc = np.clip(cols[None, :] - cols[:, None], -(NA_WIN_C - 1), NA_WIN_C - 1) + (NA_WIN_C - 1)
    sel_c = (np.arange(2 * NA_WIN_C - 1)[:, None, None] == dc[None]).astype(np.float32)
    by_col = jnp.einsum('hrc,cqk->hrqk', rpb.astype(F32), sel_c, precision=HIGHEST)
    zero = jnp.zeros((nh, 1, GRID_W, GRID_W), F32)
    padded = jnp.concatenate([zero, by_col, zero], axis=1)
    two_rows = jnp.concatenate([padded[:, :-1], padded[:, 1:]], axis=-1)
    variants = []
    for v in range(len(blocks)):
        strips = []
        for i in range(NA_QROWS):
            first = np.clip(dr[v, i, 0::2], -1, n_dr - 1) + 1
            strips.append(jnp.concatenate([two_rows[:, int(s)] for s in first], axis=-1))
        variants.append(jnp.concatenate(strips, axis=1))
    bias = jnp.stack(variants, axis=0)
    ok = (row_ok[:, :, None, :, None] & col_ok[None, None, :, None, :]).reshape(
        len(blocks), 1, NA_QROWS * GRID_W, NA_KROWS * GRID_W)
    return jnp.where(ok, bias, NEG_INF)


def _na_latent(qkv, qkv_c, bias):
    bsz, seq, w3 = qkv.shape
    width = w3 // 3
    lc = qkv_c.shape[1]
    qb = NA_QROWS * GRID_W
    nblk = seq // qb
    nkb = NA_KROWS // NA_QROWS

    def kv_spec(col, s):
        return pl.BlockSpec((1, qb, width),
                            lambda b, a: (b, jnp.clip(a - 1, 0, nblk - nkb) + s, col))

    def variant(a):
        return (a > 0).astype(jnp.int32) + (a == nblk - 1).astype(jnp.int32)

    return pl.pallas_call(
        _na_lat_kernel,
        out_shape=jax.ShapeDtypeStruct((bsz, seq, width), BF16),
        grid=(bsz, nblk),
        in_specs=[pl.BlockSpec((1, qb, width), lambda b, a: (b, a, 0))]
        + [kv_spec(1, s) for s in range(nkb)] + [kv_spec(2, s) for s in range(nkb)]
        + [pl.BlockSpec((1, lc, width), lambda b, a: (b, 0, 1)),
           pl.BlockSpec((1, lc, width), lambda b, a: (b, 0, 2)),
           pl.BlockSpec((1,) + bias.shape[1:], lambda b, a: (variant(a), 0, 0, 0))],
        out_specs=pl.BlockSpec((1, qb, width), lambda b, a: (b, a, 0)),
        compiler_params=_params(),
        name="na_latent",
    )(qkv, *([qkv] * (2 * nkb)), qkv_c, qkv_c, bias)


def _na_context(qkv_c):
    bsz, lc, w3 = qkv_c.shape
    width = w3 // 3
    return pl.pallas_call(
        _na_ctx_kernel,
        out_shape=jax.ShapeDtypeStruct((bsz, lc, width), BF16),
        grid=(bsz,),
        in_specs=[pl.BlockSpec((1, lc, width), lambda b: (b, 0, 0)),
                  pl.BlockSpec((1, lc, width), lambda b: (b, 0, 1)),
                  pl.BlockSpec((1, lc, width), lambda b: (b, 0, 2))],
        out_specs=pl.BlockSpec((1, lc, width), lambda b: (b, 0, 0)),
        compiler_params=_params(),
        name="na_context",
    )(qkv_c, qkv_c, qkv_c)


def _residual_mlp(out_l, h_ref, mod_ref, gains, w1_ref, w2_ref, o_ref):
    bsz, tt, _ = h_ref.shape
    parts = []
    for b in range(bsz):
        m = mod_ref[b]
        h1 = h_ref[b] + m[2:3] * _rms(out_l[b * tt:(b + 1) * tt], gains[1:2])
        o_ref[b] = h1
        parts.append((_rms(h1, gains[2:3]) * (1.0 + m[4:5]) + m[3:4]).astype(BF16))
    hf = jnp.concatenate(parts, axis=0)
    d_ff = w1_ref.shape[1]
    acc = None
    for kf in range(d_ff // FF_CHUNK):
        cs = slice(kf * FF_CHUNK, (kf + 1) * FF_CHUNK)
        hid = jnp.maximum(jnp.dot(hf, w1_ref[:, cs], preferred_element_type=F32), 0.0)
        part = jnp.dot((hid * hid).astype(BF16), w2_ref[cs, :], preferred_element_type=F32)
        acc = part if acc is None else acc + part
    for b in range(bsz):
        o_ref[b] = o_ref[b] + mod_ref[b][5:6] * _rms(acc[b * tt:(b + 1) * tt], gains[3:4])


def _post_even_kernel(yp_ref, na_ref, h_ref, mod_ref, g_ref, wglu_ref, wo_ref, w1_ref, w2_ref, o_ref, ys_ref):
    bsz, tt, na_w = na_ref.shape
    _pairs_to_tokens(yp_ref, ys_ref, bsz)
    g = _gelu_tanh(jnp.concatenate([ys_ref[blk] for blk in range(ys_ref.shape[0])], axis=-1))
    gate = _sigmoid(jnp.dot(g.astype(BF16), wglu_ref[...], preferred_element_type=F32))
    s5 = (g * gate).astype(BF16)
    sw = s5.shape[1]
    out_l = (jnp.dot(s5, wo_ref[0:sw, :], preferred_element_type=F32)
             + jnp.dot(na_ref[...].reshape(bsz * tt, na_w), wo_ref[sw:, :], preferred_element_type=F32))
    _residual_mlp(out_l, h_ref, mod_ref, g_ref[...], w1_ref, w2_ref, o_ref)


def _post_odd_kernel(mix_ref, h_ref, mod_ref, g_ref, wf_ref, w1_ref, w2_ref, o_ref):
    bsz, _, d = h_ref.shape
    mix = jnp.concatenate([mix_ref[:, b * d:(b + 1) * d] for b in range(bsz)], axis=0)
    out_l = jnp.dot(mix, wf_ref[...], preferred_element_type=F32)
    _residual_mlp(out_l, h_ref, mod_ref, g_ref[...], w1_ref, w2_ref, o_ref)


def _post_even(y_pairs, na, h, mod, gains, w_glu, w_out, w1, w2, layer, e):
    bsz, seq, d = h.shape
    npair = y_pairs.shape[0]
    pw = S5_T * S5_PAIR
    prow = TOK_TILE // S5_T * bsz
    return pl.pallas_call(
        _post_even_kernel,
        out_shape=jax.ShapeDtypeStruct(h.shape, F32),
        grid=_token_grid(seq),
        in_specs=[pl.BlockSpec((npair, prow, pw), lambda i: (0, i, 0)), _tok_spec(bsz, na.shape[-1]),
                  _tok_spec(bsz, d), _layer_spec(mod, layer), _layer_spec(gains, layer),
                  _layer_spec(w_glu, e), _layer_spec(w_out, e), _layer_spec(w1, layer), _layer_spec(w2, layer)],
        out_specs=_tok_spec(bsz, d),
        scratch_shapes=[pltpu.VMEM((npair * S5_PAIR // LANES, bsz * TOK_TILE, LANES), F32)],
        compiler_params=_params(),
        name="post_even",
    )(y_pairs, na, h, mod, gains, w_glu, w_out, w1, w2)


def _post_odd(mix, h, mod, gains, w_f, w1, w2, layer, o):
    bsz, seq, d = h.shape
    return pl.pallas_call(
        _post_odd_kernel,
        out_shape=jax.ShapeDtypeStruct(h.shape, F32),
        grid=_token_grid(seq),
        in_specs=[pl.BlockSpec((TOK_TILE, bsz * d), lambda i: (i, 0)), _tok_spec(bsz, d),
                  _layer_spec(mod, layer), _layer_spec(gains, layer), _layer_spec(w_f, o),
                  _layer_spec(w1, layer), _layer_spec(w2, layer)],
        out_specs=_tok_spec(bsz, d),
        compiler_params=_params(),
        name="post_odd",
    )(mix, h, mod, gains, w_f, w1, w2)


def kernel(x, c, ctx, c_ctx, w_mod, b_mod, norm_g, w_in, w_out_even, s5_lam_re, s5_lam_im, s5_log_dt,
           s5_b_re, s5_b_im, s5_c_re, s5_c_im, s5_d, s5_w_glu, na_rpb, w_fourier, w_ff1, w_ff2):
    bsz, seq, d = x.shape
    lc = ctx.shape[1]
    depth = w_mod.shape[0]
    s5w = s5_d.shape[-1]
    last_ctx_layer = 2 * ((depth - 1) // 2)

    mod_rows = 16
    a = jnp.concatenate([c, c_ctx[None, :], jnp.zeros((mod_rows - bsz - 1, d), F32)], axis=0)
    mod = _modulation(a, w_mod, b_mod)
    pad_mod = lambda m: jnp.pad(m, ((0, 0), (0, 0), (0, 8 - N_MOD), (0, 0)))
    mod_l = pad_mod(mod[:, :bsz].reshape(depth, bsz, N_MOD, d))
    mod_c = pad_mod(jnp.broadcast_to(mod[:, bsz:bsz + 1].reshape(depth, 1, N_MOD, d), (depth, bsz, N_MOD, d)))
    gains = jnp.pad(norm_g.astype(F32), ((0, 0), (0, 4), (0, 0)))

    na_w = (w_in.shape[-1] - s5w) // 3
    qscale = jnp.concatenate([jnp.ones((s5w,), F32), jnp.full((na_w,), NA_HEAD_DIM ** -0.5, F32),
                              jnp.ones((2 * na_w,), F32)])
    w_in_b = (w_in * qscale).astype(BF16)
    w_out_b = w_out_even.astype(BF16)
    w_glu_b = s5_w_glu.astype(BF16)
    w_f_b = w_fourier.astype(BF16)
    w1_b = w_ff1.astype(BF16)
    w2_b = w_ff2.astype(BF16)

    wbig, wout, dec, dsk = _s5_prepare(s5_lam_re, s5_lam_im, s5_log_dt, s5_b_re, s5_b_im,
                                       s5_c_re, s5_c_im, s5_d)
    dft_d = _dft_pair(d, right=True)
    dft_l = _dft_pair(seq, right=False)
    dft_c = _dft_pair(lc, right=False)

    h, s = x, ctx
    for layer in range(depth):
        upd_ctx = layer < last_ctx_layer
        if layer % 2 == 0:
            e = layer // 2
            u_l, qkv_l = _pre_even(h, mod_l, gains, w_in_b, layer, e)
            u_c, qkv_c = _pre_even(s, mod_c, gains, w_in_b, layer, e)
            y_c, y_l = _s5_chunked(u_c, u_l, wbig, wout, dec, dsk, e, bsz)
            bias = _na_bias_table(na_rpb[e], seq // GRID_W)
            na_l = _na_latent(qkv_l, qkv_c, bias)
            h_new = _post_even(y_l, na_l, h, mod_l, gains, w_glu_b, w_out_b, w1_b, w2_b, layer, e)
            if upd_ctx:
                na_c = _na_context(qkv_c)
                s = _post_even(y_c, na_c, s, mod_c, gains, w_glu_b, w_out_b, w1_b, w2_b, layer, e)
            h = h_new
        else:
            o = layer // 2
            ab = _pre_odd(h, mod_l, gains, dft_d, layer)
            mix = _matmul(dft_l, ab.reshape(2 * seq, bsz * d), 1024, 2048, 1024, BF16)
            h_new = _post_odd(mix, h, mod_l, gains, w_f_b, w1_b, w2_b, layer, o)
            if upd_ctx:
                ab_c = _pre_odd(s, mod_c, gains, dft_d, layer)
                mix_c = _matmul(dft_c, ab_c.reshape(2 * lc, bsz * d), lc, 2048, 2 * lc, BF16)
                s = _post_odd(mix_c, s, mod_c, gains, w_f_b, w1_b, w2_b, layer, o)
            h = h_new
    return h
```

```python
import functools
import math

import jax
import jax.numpy as jnp
import numpy as np
from jax import lax
from jax.experimental import pallas as pl
from jax.experimental.pallas import tpu as pltpu

F32 = jnp.float32
BF16 = jnp.bfloat16
EPS = 1e-6
NEG_INF = -1e30

GRID_W = 64
S5_GROUP = 16
S5_STATE = 64
S5_T = 16
S5_PAIR = 2 * S5_GROUP
LANES = 128
TOK_TILE = 64
FF_CHUNK = 1024
NA_HEAD_DIM = 64
NA_WIN_R = 8
NA_WIN_C = 16
NA_QROWS = 4
NA_KROWS = 12
N_MOD = 6

V7X_VMEM_LIMIT_BYTES = 56 * 1024 * 1024
HIGHEST = lax.Precision.HIGHEST


def _params(**kw):
    return pltpu.CompilerParams(vmem_limit_bytes=V7X_VMEM_LIMIT_BYTES, **kw)


def _resident(shape):
    nd = len(shape)
    return pl.BlockSpec(shape, lambda *_: (0,) * nd, pipeline_mode=pl.Buffered(1))


def _layer_spec(stacked, idx):
    nd = stacked.ndim
    return pl.BlockSpec((None,) + stacked.shape[1:], lambda *_: (idx,) + (0,) * (nd - 1),
                        pipeline_mode=pl.Buffered(1))


def _rms(x, g):
    ms = jnp.mean(x * x, axis=-1, keepdims=True)
    return x * lax.rsqrt(ms + EPS) * g


def _sigmoid(x):
    return 1.0 / (1.0 + jnp.exp(-x))


def _gelu_tanh(x):
    c = math.sqrt(2.0 / math.pi)
    return x * (0.5 * (1.0 + jnp.tanh(c * (x + 0.044715 * (x * x * x)))))


def _mod_kernel(a_ref, w_ref, b_ref, o_ref):
    a = a_ref[...]
    act = a * _sigmoid(a)
    o_ref[0] = jnp.dot(act.astype(BF16), w_ref[0].astype(BF16), preferred_element_type=F32) + b_ref[0]


def _modulation(a, w_mod, b_mod):
    depth, d, n = w_mod.shape
    rows = a.shape[0]
    tn = 1536
    return pl.pallas_call(
        _mod_kernel,
        out_shape=jax.ShapeDtypeStruct((depth, rows, n), F32),
        grid=(depth, n // tn),
        in_specs=[
            pl.BlockSpec((rows, d), lambda l, j: (0, 0)),
            pl.BlockSpec((1, d, tn), lambda l, j: (l, 0, j)),
            pl.BlockSpec((1, 1, tn), lambda l, j: (l, 0, j)),
        ],
        out_specs=pl.BlockSpec((1, rows, tn), lambda l, j: (l, 0, j)),
        compiler_params=_params(),
        name="adaln_mod",
    )(a, w_mod, b_mod.reshape(depth, 1, n))


def _lane_quarter(shape):
    return lax.broadcasted_iota(jnp.int32, shape, 1) // S5_PAIR


def _tokens_to_pairs(tok_ref, pair_ref, bsz):
    per = LANES // S5_PAIR
    tt = tok_ref.shape[1] // bsz
    n_chunks = tt // S5_T
    quarter = _lane_quarter((n_chunks * bsz, LANES))
    for blk in range(tok_ref.shape[0]):
        for m in range(S5_T // per):
            src = [jnp.concatenate([tok_ref[blk, pl.ds(c * S5_T + per * m + jj, bsz, stride=tt), :]
                                    for c in range(n_chunks)], axis=0) for jj in range(per)]
            for qq in range(per):
                dest = None
                for jj in range(per):
                    shift = ((jj - qq) % per) * S5_PAIR
                    r = src[jj] if shift == 0 else pltpu.roll(src[jj], shift, axis=1)
                    dest = r if dest is None else jnp.where(quarter == jj, r, dest)
                pair_ref[per * blk + qq, :, m * LANES:(m + 1) * LANES] = dest


def _pairs_to_tokens(pair_ref, tok_ref, bsz):
    per = LANES // S5_PAIR
    tt = tok_ref.shape[1] // bsz
    n_chunks = tt // S5_T
    quarter = _lane_quarter((n_chunks * bsz, LANES))
    for blk in range(tok_ref.shape[0]):
        for m in range(S5_T // per):
            src = [pair_ref[per * blk + qq, :, m * LANES:(m + 1) * LANES] for qq in range(per)]
            for ts in range(per):
                dest = None
                for qq in range(per):
                    shift = ((qq - ts) % per) * S5_PAIR
                    r = src[qq] if shift == 0 else pltpu.roll(src[qq], shift, axis=1)
                    dest = r if dest is None else jnp.where(quarter == qq, r, dest)
                for c in range(n_chunks):
                    tok_ref[blk, pl.ds(c * S5_T + per * m + ts, bsz, stride=tt), :] = dest[c * bsz:(c + 1) * bsz]


def _modulated(h_ref, mod_ref, gain, shift_row, scale_row):
    parts = []
    for b in range(h_ref.shape[0]):
        m = mod_ref[b]
        y = _rms(h_ref[b], gain) * (1.0 + m[scale_row:scale_row + 1]) + m[shift_row:shift_row + 1]
        parts.append(y.astype(BF16))
    return jnp.concatenate(parts, axis=0)


def _pre_even_kernel(h_ref, mod_ref, g_ref, w_ref, up_ref, qkv_ref, us_ref, *, s5_width):
    bsz, tt, _ = h_ref.shape
    hl = _modulated(h_ref, mod_ref, g_ref[0:1], 0, 1)
    z = jnp.dot(hl, w_ref[...], preferred_element_type=F32)
    for blk in range(s5_width // LANES):
        us_ref[blk] = z[:, blk * LANES:(blk + 1) * LANES]
    for b in range(bsz):
        qkv_ref[b] = z[b * tt:(b + 1) * tt, s5_width:].astype(BF16)
    _tokens_to_pairs(us_ref, up_ref, bsz)


def _token_grid(seq):
    return (seq // TOK_TILE,)


def _tok_spec(bsz, width):
    return pl.BlockSpec((bsz, TOK_TILE, width), lambda i: (0, i, 0))


def _pre_even(h, mod, gains, w_in, layer, e):
    bsz, seq, d = h.shape
    n = w_in.shape[-1]
    s5w = d // 2
    npair = s5w // S5_PAIR
    pw = S5_T * S5_PAIR
    prow = TOK_TILE // S5_T * bsz
    return pl.pallas_call(
        functools.partial(_pre_even_kernel, s5_width=s5w),
        out_shape=(jax.ShapeDtypeStruct((npair, seq // S5_T * bsz, pw), F32),
                   jax.ShapeDtypeStruct((bsz, seq, n - s5w), BF16)),
        grid=_token_grid(seq),
        in_specs=[_tok_spec(bsz, d), _layer_spec(mod, layer), _layer_spec(gains, layer), _layer_spec(w_in, e)],
        out_specs=(pl.BlockSpec((npair, prow, pw), lambda i: (0, i, 0)), _tok_spec(bsz, n - s5w)),
        scratch_shapes=[pltpu.VMEM((s5w // LANES, bsz * TOK_TILE, LANES), F32)],
        compiler_params=_params(),
        name="pre_even",
    )(h, mod, gains, w_in)


def _pre_odd_kernel(h_ref, mod_ref, g_ref, w_ref, a_ref, b_ref, *, half):
    bsz, tt, _ = h_ref.shape
    hl = _modulated(h_ref, mod_ref, g_ref[0:1], 0, 1)
    z = jnp.dot(hl, w_ref[...], preferred_element_type=F32)
    lane = lax.broadcasted_iota(jnp.int32, (tt, LANES), 1)
    nyq = jnp.zeros((tt, LANES), F32)
    for b in range(bsz):
        zb = z[b * tt:(b + 1) * tt]
        a_ref[:, b * half:(b + 1) * half] = zb[:, :half].astype(BF16)
        b_ref[:, b * half:(b + 1) * half] = zb[:, half:2 * half].astype(BF16)
        nyq = jnp.where(lane == b, zb[:, 2 * half:], nyq)
    a_ref[:, bsz * half:] = nyq.astype(BF16)


def _pre_odd(h, mod, gains, w_cs, layer):
    bsz, seq, d = h.shape
    half = d // 2
    return pl.pallas_call(
        functools.partial(_pre_odd_kernel, half=half),
        out_shape=(jax.ShapeDtypeStruct((seq, bsz * half + LANES), BF16),
                   jax.ShapeDtypeStruct((seq, bsz * half), BF16)),
        grid=_token_grid(seq),
        in_specs=[_tok_spec(bsz, d), _layer_spec(mod, layer), _layer_spec(gains, layer), _resident(w_cs.shape)],
        out_specs=(pl.BlockSpec((TOK_TILE, bsz * half + LANES), lambda i: (i, 0)),
                   pl.BlockSpec((TOK_TILE, bsz * half), lambda i: (i, 0))),
        compiler_params=_params(),
        name="pre_odd",
    )(h, mod, gains, w_cs)


def _matmul_kernel(a_ref, b_ref, o_ref, acc_ref, *, nk):
    k = pl.program_id(2)

    @pl.when(k == 0)
    def _():
        acc_ref[...] = jnp.zeros_like(acc_ref)

    acc_ref[...] += jnp.dot(a_ref[...], b_ref[...], preferred_element_type=F32)

    @pl.when(k == nk - 1)
    def _():
        o_ref[...] = acc_ref[...].astype(o_ref.dtype)


def _matmul(a, b, tm, tn, tk, out_dtype, a_col0=0):
    m = a.shape[0]
    kdim, n = b.shape
    nk = kdim // tk
    return pl.pallas_call(
        functools.partial(_matmul_kernel, nk=nk),
        out_shape=jax.ShapeDtypeStruct((m, n), out_dtype),
        grid=(m // tm, n // tn, nk),
        in_specs=[pl.BlockSpec((tm, tk), lambda i, j, k: (i, k + a_col0)),
                  pl.BlockSpec((tk, tn), lambda i, j, k: (k, j))],
        out_specs=pl.BlockSpec((tm, tn), lambda i, j, k: (i, j)),
        scratch_shapes=[pltpu.VMEM((tm, tn), F32)],
        compiler_params=_params(),
        name="fnet_seq_dft",
    )(a, b)


def _dft_pair(n, right):
    n0 = LANES
    n1 = n // n0
    k = jnp.arange(n, dtype=jnp.int32)[:, None]
    pa = (k * (jnp.arange(n1, dtype=jnp.int32)[None, :] * n0)) % n
    pb = (k * jnp.arange(n0, dtype=jnp.int32)[None, :]) % n
    w = 2.0 * math.pi / n
    scale = 1.0 / math.sqrt(n)
    sign = 1.0 if right else -1.0
    ca, sa = jnp.cos(pa.astype(F32) * w) * scale, jnp.sin(pa.astype(F32) * w) * scale
    cb, sb = jnp.cos(pb.astype(F32) * w), jnp.sin(pb.astype(F32) * w)
    a1 = jnp.concatenate([ca, sign * sa], axis=1)
    a2 = jnp.concatenate([sa, -sign * ca], axis=1)
    tr = min(n, 256)
    rows = lambda w_: pl.BlockSpec((tr, w_), lambda i: (i, 0))
    return pl.pallas_call(
        _dft_expand_kernel,
        out_shape=jax.ShapeDtypeStruct((n, 2 * n), BF16),
        grid=(n // tr,),
        in_specs=[rows(2 * n1), rows(2 * n1), rows(n0), rows(n0)],
        out_specs=rows(2 * n),
        compiler_params=_params(),
        name="dft_expand",
    )(a1, a2, cb, sb)


def _dft_expand_kernel(a1_ref, a2_ref, cb_ref, sb_ref, o_ref):
    cb, sb = cb_ref[...], sb_ref[...]
    for j in range(a1_ref.shape[1]):
        blk = a1_ref[:, j:j + 1] * cb - a2_ref[:, j:j + 1] * sb
        o_ref[:, j * LANES:(j + 1) * LANES] = blk.astype(BF16)


def _s5_prepare(lam_re, lam_im, log_dt, b_re, b_im, c_re, c_im, d_skip):
    t_len, hh, pp = S5_T, S5_GROUP, S5_STATE
    ne, _, gg, _ = lam_re.shape
    lam_re = jnp.minimum(lam_re.astype(F32), -1e-4)
    lam_im = lam_im.astype(F32)
    dt = jnp.exp(log_dt.astype(F32))[..., None]
    mag = jnp.exp(lam_re * dt)
    a_re = mag * jnp.cos(lam_im * dt)
    a_im = mag * jnp.sin(lam_im * dt)
    den = lam_re * lam_re + lam_im * lam_im
    num_re = a_re - 1.0
    f_re = (num_re * lam_re + a_im * lam_im) / den
    f_im = (a_im * lam_re - num_re * lam_im) / den
    b_re = b_re.astype(F32)
    b_im = b_im.astype(F32)
    bb_re = f_re[..., None] * b_re - f_im[..., None] * b_im
    bb_im = f_re[..., None] * b_im + f_im[..., None] * b_re
    tau = jnp.arange(t_len + 1, dtype=F32).reshape(1, 1, 1, -1, 1)
    lr, li, dtt = lam_re[..., None, :], lam_im[..., None, :], dt[..., None]
    pmag = jnp.exp(lr * dtt * tau)
    p_re = pmag * jnp.cos(li * dtt * tau)
    p_im = pmag * jnp.sin(li * dtt * tau)
    bbt_re = jnp.swapaxes(bb_re, -1, -2)[:, :, :, None]
    bbt_im = jnp.swapaxes(bb_im, -1, -2)[:, :, :, None]
    abt_re = p_re[..., None, :] * bbt_re - p_im[..., None, :] * bbt_im
    abt_im = p_re[..., None, :] * bbt_im + p_im[..., None, :] * bbt_re
    c_re = c_re.astype(F32)[:, :, :, None]
    c_im = c_im.astype(F32)[:, :, :, None]
    e_re = c_re * p_re[..., None, :] - c_im * p_im[..., None, :]
    e_im = c_re * p_im[..., None, :] + c_im * p_re[..., None, :]

    gp = gg // 2
    pw = t_len * 2 * hh

    def cat(re, im):
        tn = re.shape[2]
        re = re.reshape(ne, gp, 2, tn, hh, pp)
        im = im.reshape(ne, gp, 2, tn, hh, pp)
        z = jnp.zeros((ne, gp, tn, hh, pp), F32)
        e0 = jnp.concatenate([re[:, :, 0], z, im[:, :, 0], z], axis=-1)
        e1 = jnp.concatenate([z, re[:, :, 1], z, im[:, :, 1]], axis=-1)
        return jnp.stack([e0, e1], axis=3).reshape(ne, gp, tn * 2 * hh, 4 * pp)

    fwd_t = slice(0, t_len)
    rev_t = slice(t_len - 1, None, -1)
    st_cols, out_rows, resp = [], [], []
    for d in range(2):
        tau_k = fwd_t if d == 0 else rev_t
        bcat0 = cat(abt_re[:, d, :, 0:1], abt_im[:, d, :, 0:1])
        ecat = cat(e_re[:, d, :, tau_k], -e_im[:, d, :, tau_k])
        resp.append(jnp.einsum('xqrk,xqck->xqrc', bcat0, ecat, precision=HIGHEST))
        tau_s = rev_t if d == 0 else fwd_t
        st_cols.append(cat(abt_re[:, d, :, tau_s], abt_im[:, d, :, tau_s]))
        tau_o = slice(1, t_len + 1) if d == 0 else slice(t_len, 0, -1)
        out_rows.append(jnp.swapaxes(cat(e_re[:, d, :, tau_o], -e_im[:, d, :, tau_o]), -1, -2))

    span = pw - 2 * hh
    kf = jnp.pad(resp[0], ((0, 0), (0, 0), (0, 0), (span, 0)))
    kb = jnp.pad(resp[1], ((0, 0), (0, 0), (0, 0), (0, span)))
    blocks = []
    for j in range(t_len):
        lo_f = span - 2 * hh * j
        lo_b = 2 * hh * (t_len - 1 - j)
        blocks.append(kf[..., lo_f:lo_f + pw] + kb[..., lo_b:lo_b + pw])
    toep = jnp.stack(blocks, axis=2).reshape(ne, gp, pw, pw)
    wbig = jnp.concatenate([toep] + st_cols, axis=-1).astype(BF16)
    wout = jnp.concatenate(out_rows, axis=2).astype(BF16)

    dec_rows = [p_re[:, 0, :, t_len], p_im[:, 0, :, t_len], p_re[:, 1, :, t_len], p_im[:, 1, :, t_len]]
    dec = jnp.stack([x.reshape(ne, gp, 2 * pp) for x in dec_rows], axis=2)
    dec = jnp.pad(dec, ((0, 0), (0, 0), (0, 4), (0, 0)))
    dsk = jnp.broadcast_to(d_skip.astype(F32).reshape(ne, gp, 1, 2 * hh), (ne, gp, t_len, 2 * hh))
    dsk = jnp.pad(dsk.reshape(ne, gp, 1, pw), ((0, 0), (0, 0), (0, 7), (0, 0)))
    return wbig, wout, dec, dsk


def _s5_kernel(uc_ref, ul_ref, wbig_ref, wout_ref, dec_ref, dsk_ref, yc_ref, yl_ref, z_ref, sin_ref, *, bsz):
    rows_c, rows_l = uc_ref.shape[1], ul_ref.shape[1]
    n_ctx, n_chunks = rows_c // bsz, (rows_c + rows_l) // bsz
    tr = 128
    wy = S5_T * S5_PAIR
    sw = 2 * S5_STATE
    parts = ((uc_ref, yc_ref, 0, rows_c), (ul_ref, yl_ref, rows_c, rows_l))

    for u_ref, _, base, nrows in parts:
        def stage1(i, carry, u_ref=u_ref, base=base):
            r = pl.multiple_of(i * tr, tr)
            ub = u_ref[0, pl.ds(r, tr), :].astype(BF16)
            z_ref[pl.ds(base + r, tr), :] = jnp.dot(ub, wbig_ref[0], preferred_element_type=F32)
            return carry

        lax.fori_loop(0, nrows // tr, stage1, 0)

    dec = dec_ref[0]
    zero = jnp.zeros((bsz, sw), F32)

    def make_step(d):
        a_re = dec[2 * d:2 * d + 1]
        a_im = dec[2 * d + 1:2 * d + 2]
        lc = wy + 2 * d * sw
        sc = 2 * d * sw

        def step(c, carry):
            s_re, s_im = carry
            r = pl.multiple_of(c * bsz, bsz)
            sin_ref[pl.ds(r, bsz), sc:sc + sw] = s_re
            sin_ref[pl.ds(r, bsz), sc + sw:sc + 2 * sw] = s_im
            l_re = z_ref[pl.ds(r, bsz), lc:lc + sw]
            l_im = z_ref[pl.ds(r, bsz), lc + sw:lc + 2 * sw]
            return (a_re * s_re - a_im * s_im + l_re, a_re * s_im + a_im * s_re + l_im)

        return step

    fwd, bwd = make_step(0), make_step(1)

    def both(k, carry):
        cb = jnp.where(k < n_ctx, n_ctx - 1 - k, n_chunks + n_ctx - 1 - k)
        return fwd(k, carry[:2]) + bwd(cb, carry[2:])

    lax.fori_loop(0, n_chunks, both, (zero, zero, zero, zero), unroll=2)

    dsk = dsk_ref[0][0:1]

    for u_ref, y_ref, base, nrows in parts:
        def stage3(i, carry, u_ref=u_ref, y_ref=y_ref, base=base):
            r = pl.multiple_of(i * tr, tr)
            sb = sin_ref[pl.ds(base + r, tr), :].astype(BF16)
            y = (dsk * u_ref[0, pl.ds(r, tr), :] + z_ref[pl.ds(base + r, tr), 0:wy]
                 + jnp.dot(sb, wout_ref[0], preferred_element_type=F32))
            y_ref[0, pl.ds(r, tr), :] = y
            return carry

        lax.fori_loop(0, nrows // tr, stage3, 0)


def _s5_chunked(u_ctx, u_lat, wbig, wout, dec, dsk, e, bsz):
    gp, rows_c, width = u_ctx.shape
    rows_l = u_lat.shape[1]
    zc = wbig.shape[-1]
    per_pair = lambda shape: pl.BlockSpec((1,) + shape[1:], lambda q: (q, 0, 0))
    of_layer = lambda w: pl.BlockSpec((None, 1) + w.shape[2:], lambda q: (e, q, 0, 0))
    return pl.pallas_call(
        functools.partial(_s5_kernel, bsz=bsz),
        out_shape=(jax.ShapeDtypeStruct(u_ctx.shape, F32), jax.ShapeDtypeStruct(u_lat.shape, F32)),
        grid=(gp,),
        in_specs=[per_pair(u_ctx.shape), per_pair(u_lat.shape), of_layer(wbig), of_layer(wout),
                  of_layer(dec), of_layer(dsk)],
        out_specs=(per_pair(u_ctx.shape), per_pair(u_lat.shape)),
        scratch_shapes=[pltpu.VMEM((rows_c + rows_l, zc), F32), pltpu.VMEM((rows_c + rows_l, width), F32)],
        compiler_params=_params(),
        name="s5_chunked",
    )(u_ctx, u_lat, wbig, wout, dec, dsk)


def _attend_pairs(q_ref, k_refs, v_refs, bias_ref, o_ref, n_biased):
    width = q_ref.shape[-1]
    kb = k_refs[0].shape[1]
    lane = lax.broadcasted_iota(jnp.int32, (1, 2 * NA_HEAD_DIM), 1)
    outs = []
    for hp in range(width // (2 * NA_HEAD_DIM)):
        cs = slice(hp * 2 * NA_HEAD_DIM, (hp + 1) * 2 * NA_HEAD_DIM)
        qp = q_ref[0, :, cs]
        ks = [r[0, :, cs] for r in k_refs]
        vs = [r[0, :, cs] for r in v_refs]
        res = []
        for e in range(2):
            sel = (lane // NA_HEAD_DIM) == e
            qe = jnp.where(sel, qp, jnp.zeros_like(qp))
            parts = []
            for i, kk in enumerate(ks):
                s = lax.dot_general(qe, kk, (((1,), (1,)), ((), ())), preferred_element_type=F32)
                if i < n_biased:
                    s = s + bias_ref[0, 2 * hp + e, :, i * kb:(i + 1) * kb]
                parts.append(s)
            m = parts[0].max(axis=-1, keepdims=True)
            for s in parts[1:]:
                m = jnp.maximum(m, s.max(axis=-1, keepdims=True))
            ps = [jnp.exp(s - m) for s in parts]
            den = ps[0].sum(axis=-1, keepdims=True)
            for p in ps[1:]:
                den = den + p.sum(axis=-1, keepdims=True)
            acc = jnp.dot(ps[0].astype(BF16), vs[0], preferred_element_type=F32)
            for p, vv in zip(ps[1:], vs[1:]):
                acc = acc + jnp.dot(p.astype(BF16), vv, preferred_element_type=F32)
            res.append(acc / den)
        outs.append(jnp.where(lane < NA_HEAD_DIM, res[0], res[1]))
    o_ref[0] = jnp.concatenate(outs, axis=-1).astype(o_ref.dtype)


def _na_lat_kernel(q_ref, k0, k1, k2, v0, v1, v2, kc, vc, bias_ref, o_ref):
    _attend_pairs(q_ref, (k0, k1, k2, kc), (v0, v1, v2, vc), bias_ref, o_ref, 3)


def _na_ctx_kernel(q_ref, kc, vc, o_ref):
    _attend_pairs(q_ref, (kc,), (vc,), None, o_ref, 0)


def _na_bias_table(rpb, rows):
    nh = rpb.shape[0]
    nblk = rows // NA_QROWS
    blocks = np.array([0, 1, nblk - 1])
    kb0 = np.clip(blocks - 1, 0, nblk - NA_KROWS // NA_QROWS) * NA_QROWS
    qr = blocks[:, None] * NA_QROWS + np.arange(NA_QROWS)[None, :]
    r0 = np.clip(qr - NA_WIN_R // 2, 0, rows - NA_WIN_R)
    kr = kb0[:, None] + np.arange(NA_KROWS)[None, :]
    row_ok = (kr[:, None, :] >= r0[:, :, None]) & (kr[:, None, :] < r0[:, :, None] + NA_WIN_R)
    n_dr = 2 * NA_WIN_R - 1
    dr = kr[:, None, :] - qr[:, :, None] + (NA_WIN_R - 1)
    cols = np.arange(GRID_W)
    c0 = np.clip(cols - NA_WIN_C // 2, 0, GRID_W - NA_WIN_C)
    col_ok = (cols[None, :] >= c0[:, None]) & (cols[None, :] < c0[:, None] + NA_WIN_C)
    dc = np.clip(cols[None, :] - cols[:, None], -(NA_WIN_C - 1), NA_WIN_C - 1) + (NA_WIN_C - 1)
    sel_c = (np.arange(2 * NA_WIN_C - 1)[:, None, None] == dc[None]).astype(np.float32)
    by_col = jnp.einsum('hrc,cqk->hrqk', rpb.astype(F32), sel_c, precision=HIGHEST)
    zero = jnp.zeros((nh, 1, GRID_W, GRID_W), F32)
    padded = jnp.concatenate([zero, by_col, zero], axis=1)
    two_rows = jnp.concatenate([padded[:, :-1], padded[:, 1:]], axis=-1)
    variants = []
    for v in range(len(blocks)):
        strips = []
        for i in range(NA_QROWS):
            first = np.clip(dr[v, i, 0::2], -1, n_dr - 1) + 1
            strips.append(jnp.concatenate([two_rows[:, int(s)] for s in first], axis=-1))
        variants.append(jnp.concatenate(strips, axis=1))
    bias = jnp.stack(variants, axis=0)
    ok = (row_ok[:, :, None, :, None] & col_ok[None, None, :, None, :]).reshape(
        len(blocks), 1, NA_QROWS * GRID_W, NA_KROWS * GRID_W)
    return jnp.where(ok, bias, NEG_INF)


def _na_latent(qkv, qkv_c, bias):
    bsz, seq, w3 = qkv.shape
    width = w3 // 3
    lc = qkv_c.shape[1]
    qb = NA_QROWS * GRID_W
    nblk = seq // qb
    nkb = NA_KROWS // NA_QROWS

    def kv_spec(col, s):
        return pl.BlockSpec((1, qb, width),
                            lambda b, a: (b, jnp.clip(a - 1, 0, nblk - nkb) + s, col))

    def variant(a):
        return (a > 0).astype(jnp.int32) + (a == nblk - 1).astype(jnp.int32)

    return pl.pallas_call(
        _na_lat_kernel,
        out_shape=jax.ShapeDtypeStruct((bsz, seq, width), BF16),
        grid=(bsz, nblk),
        in_specs=[pl.BlockSpec((1, qb, width), lambda b, a: (b, a, 0))]
        + [kv_spec(1, s) for s in range(nkb)] + [kv_spec(2, s) for s in range(nkb)]
        + [pl.BlockSpec((1, lc, width), lambda b, a: (b, 0, 1)),
           pl.BlockSpec((1, lc, width), lambda b, a: (b, 0, 2)),
           pl.BlockSpec((1,) + bias.shape[1:], lambda b, a: (variant(a), 0, 0, 0))],
        out_specs=pl.BlockSpec((1, qb, width), lambda b, a: (b, a, 0)),
        compiler_params=_params(),
        name="na_latent",
    )(qkv, *([qkv] * (2 * nkb)), qkv_c, qkv_c, bias)


def _na_context(qkv_c):
    bsz, lc, w3 = qkv_c.shape
    width = w3 // 3
    return pl.pallas_call(
        _na_ctx_kernel,
        out_shape=jax.ShapeDtypeStruct((bsz, lc, width), BF16),
        grid=(bsz,),
        in_specs=[pl.BlockSpec((1, lc, width), lambda b: (b, 0, 0)),
                  pl.BlockSpec((1, lc, width), lambda b: (b, 0, 1)),
                  pl.BlockSpec((1, lc, width), lambda b: (b, 0, 2))],
        out_specs=pl.BlockSpec((1, lc, width), lambda b: (b, 0, 0)),
        compiler_params=_params(),
        name="na_context",
    )(qkv_c, qkv_c, qkv_c)


def _residual_mlp(out_l, h_ref, mod_ref, gains, w1_ref, w2_ref, o_ref):
    bsz, tt, _ = h_ref.shape
    parts = []
    for b in range(bsz):
        m = mod_ref[b]
        h1 = h_ref[b] + m[2:3] * _rms(out_l[b * tt:(b + 1) * tt], gains[1:2])
        o_ref[b] = h1
        parts.append((_rms(h1, gains[2:3]) * (1.0 + m[4:5]) + m[3:4]).astype(BF16))
    hf = jnp.concatenate(parts, axis=0)
    d_ff = w1_ref.shape[1]
    acc = None
    for kf in range(d_ff // FF_CHUNK):
        cs = slice(kf * FF_CHUNK, (kf + 1) * FF_CHUNK)
        hid = jnp.maximum(jnp.dot(hf, w1_ref[:, cs], preferred_element_type=F32), 0.0)
        part = jnp.dot((hid * hid).astype(BF16), w2_ref[cs, :], preferred_element_type=F32)
        acc = part if acc is None else acc + part
    for b in range(bsz):
        o_ref[b] = o_ref[b] + mod_ref[b][5:6] * _rms(acc[b * tt:(b + 1) * tt], gains[3:4])


def _post_even_kernel(yp_ref, na_ref, h_ref, mod_ref, g_ref, wglu_ref, wo_ref, w1_ref, w2_ref, o_ref, ys_ref):
    bsz, tt, na_w = na_ref.shape
    _pairs_to_tokens(yp_ref, ys_ref, bsz)
    g = _gelu_tanh(jnp.concatenate([ys_ref[blk] for blk in range(ys_ref.shape[0])], axis=-1))
    gate = _sigmoid(jnp.dot(g.astype(BF16), wglu_ref[...], preferred_element_type=F32))
    s5 = (g * gate).astype(BF16)
    sw = s5.shape[1]
    out_l = (jnp.dot(s5, wo_ref[0:sw, :], preferred_element_type=F32)
             + jnp.dot(na_ref[...].reshape(bsz * tt, na_w), wo_ref[sw:, :], preferred_element_type=F32))
    _residual_mlp(out_l, h_ref, mod_ref, g_ref[...], w1_ref, w2_ref, o_ref)


def _post_odd_kernel(p_ref, q_ref, h_ref, mod_ref, g_ref, wp_ref, wq_ref, wn_ref, w1_ref, w2_ref, o_ref):
    bsz, tt, _ = h_ref.shape
    half = wp_ref.shape[0]
    pc = jnp.concatenate([p_ref[:, b * half:(b + 1) * half] for b in range(bsz)], axis=0)
    qc = jnp.concatenate([q_ref[:, b * half:(b + 1) * half] for b in range(bsz)], axis=0)
    out_l = (jnp.dot(pc, wp_ref[...], preferred_element_type=F32)
             + jnp.dot(qc, wq_ref[...], preferred_element_type=F32))
    lane = lax.broadcasted_iota(jnp.int32, (tt, LANES), 1)
    nyq_blk = p_ref[:, bsz * half:].astype(F32)
    nyq = jnp.concatenate([jnp.sum(jnp.where(lane == b, nyq_blk, 0.0), axis=1, keepdims=True)
                           for b in range(bsz)], axis=0)
    out_l = out_l + nyq * wn_ref[0:1, :]
    _residual_mlp(out_l, h_ref, mod_ref, g_ref[...], w1_ref, w2_ref, o_ref)


def _post_even(y_pairs, na, h, mod, gains, w_glu, w_out, w1, w2, layer, e):
    bsz, seq, d = h.shape
    npair = y_pairs.shape[0]
    pw = S5_T * S5_PAIR
    prow = TOK_TILE // S5_T * bsz
    return pl.pallas_call(
        _post_even_kernel,
        out_shape=jax.ShapeDtypeStruct(h.shape, F32),
        grid=_token_grid(seq),
        in_specs=[pl.BlockSpec((npair, prow, pw), lambda i: (0, i, 0)), _tok_spec(bsz, na.shape[-1]),
                  _tok_spec(bsz, d), _layer_spec(mod, layer), _layer_spec(gains, layer),
                  _layer_spec(w_glu, e), _layer_spec(w_out, e), _layer_spec(w1, layer), _layer_spec(w2, layer)],
        out_specs=_tok_spec(bsz, d),
        scratch_shapes=[pltpu.VMEM((npair * S5_PAIR // LANES, bsz * TOK_TILE, LANES), F32)],
        compiler_params=_params(),
        name="post_even",
    )(y_pairs, na, h, mod, gains, w_glu, w_out, w1, w2)


def _post_odd(p, q, h, mod, gains, w_p, w_q, w_n, w1, w2, layer, o):
    bsz, seq, d = h.shape
    rows = lambda arr: pl.BlockSpec((TOK_TILE, arr.shape[1]), lambda i: (i, 0))
    return pl.pallas_call(
        _post_odd_kernel,
        out_shape=jax.ShapeDtypeStruct(h.shape, F32),
        grid=_token_grid(seq),
        in_specs=[rows(p), rows(q), _tok_spec(bsz, d),
                  _layer_spec(mod, layer), _layer_spec(gains, layer),
                  _layer_spec(w_p, o), _layer_spec(w_q, o), _layer_spec(w_n, o),
                  _layer_spec(w1, layer), _layer_spec(w2, layer)],
        out_specs=_tok_spec(bsz, d),
        compiler_params=_params(),
        name="post_odd",
    )(p, q, h, mod, gains, w_p, w_q, w_n, w1, w2)


def _fnet_seq_dft(dft, a, b, tm, tk):
    seq = dft.shape[0]
    p = _matmul(dft, a, tm, a.shape[1] // 3, tk, BF16)
    q = _matmul(dft, b, tm, b.shape[1] // 2, tk, BF16, a_col0=seq // tk)
    return p, q


def kernel(x, c, ctx, c_ctx, w_mod, b_mod, norm_g, w_in, w_out_even, s5_lam_re, s5_lam_im, s5_log_dt,
           s5_b_re, s5_b_im, s5_c_re, s5_c_im, s5_d, s5_w_glu, na_rpb, w_fourier, w_ff1, w_ff2):
    bsz, seq, d = x.shape
    lc = ctx.shape[1]
    depth = w_mod.shape[0]
    s5w = s5_d.shape[-1]
    last_ctx_layer = 2 * ((depth - 1) // 2)

    mod_rows = 16
    a = jnp.concatenate([c, c_ctx[None, :], jnp.zeros((mod_rows - bsz - 1, d), F32)], axis=0)
    mod = _modulation(a, w_mod, b_mod)
    pad_mod = lambda m: jnp.pad(m, ((0, 0), (0, 0), (0, 8 - N_MOD), (0, 0)))
    mod_l = pad_mod(mod[:, :bsz].reshape(depth, bsz, N_MOD, d))
    mod_c = pad_mod(jnp.broadcast_to(mod[:, bsz:bsz + 1].reshape(depth, 1, N_MOD, d), (depth, bsz, N_MOD, d)))
    gains = jnp.pad(norm_g.astype(F32), ((0, 0), (0, 4), (0, 0)))

    na_w = (w_in.shape[-1] - s5w) // 3
    qscale = jnp.concatenate([jnp.ones((s5w,), F32), jnp.full((na_w,), NA_HEAD_DIM ** -0.5, F32),
                              jnp.ones((2 * na_w,), F32)])
    w_in_b = (w_in * qscale).astype(BF16)
    w_out_b = w_out_even.astype(BF16)
    w_glu_b = s5_w_glu.astype(BF16)
    half = d // 2
    wf_lo = w_fourier[:, 1:half]
    wf_hi = w_fourier[:, half + 1:][:, ::-1]
    w_p_b = jnp.concatenate([w_fourier[:, 0:1], wf_lo + wf_hi], axis=1).astype(BF16)
    w_q_b = jnp.concatenate([jnp.zeros_like(w_fourier[:, 0:1]), wf_lo - wf_hi], axis=1).astype(BF16)
    w_n = jnp.pad(w_fourier[:, half:half + 1].astype(F32), ((0, 0), (0, 7), (0, 0)))
    w1_b = w_ff1.astype(BF16)
    w2_b = w_ff2.astype(BF16)

    wbig, wout, dec, dsk = _s5_prepare(s5_lam_re, s5_lam_im, s5_log_dt, s5_b_re, s5_b_im,
                                       s5_c_re, s5_c_im, s5_d)
    dft_d = _dft_pair(d, right=True)
    dft_dh = jnp.concatenate([dft_d[:, :half], dft_d[:, d:d + half],
                              jnp.broadcast_to(dft_d[:, half:half + 1], (d, LANES))], axis=1)
    dft_l = _dft_pair(seq, right=False)
    dft_c = _dft_pair(lc, right=False)

    h, s = x, ctx
    for layer in range(depth):
        upd_ctx = layer < last_ctx_layer
        if layer % 2 == 0:
            e = layer // 2
            u_l, qkv_l = _pre_even(h, mod_l, gains, w_in_b, layer, e)
            u_c, qkv_c = _pre_even(s, mod_c, gains, w_in_b, layer, e)
            y_c, y_l = _s5_chunked(u_c, u_l, wbig, wout, dec, dsk, e, bsz)
            bias = _na_bias_table(na_rpb[e], seq // GRID_W)
            na_l = _na_latent(qkv_l, qkv_c, bias)
            h_new = _post_even(y_l, na_l, h, mod_l, gains, w_glu_b, w_out_b, w1_b, w2_b, layer, e)
            if upd_ctx:
                na_c = _na_context(qkv_c)
                s = _post_even(y_c, na_c, s, mod_c, gains, w_glu_b, w_out_b, w1_b, w2_b, layer, e)
            h = h_new
        else:
            o = layer // 2
            a_l, b_l = _pre_odd(h, mod_l, gains, dft_dh, layer)
            p_l, q_l = _fnet_seq_dft(dft_l, a_l, b_l, 1024, 1024)
            h_new = _post_odd(p_l, q_l, h, mod_l, gains, w_p_b, w_q_b, w_n, w1_b, w2_b, layer, o)
            if upd_ctx:
                a_c, b_c = _pre_odd(s, mod_c, gains, dft_dh, layer)
                p_c, q_c = _fnet_seq_dft(dft_c, a_c, b_c, lc, lc)
                s = _post_odd(p_c, q_c, s, mod_c, gains, w_p_b, w_q_b, w_n, w1_b, w2_b, layer, o)
            h = h_new
    return h
```

```python
import functools
import math

import jax
import jax.numpy as jnp
import numpy as np
from jax import lax
from jax.experimental import pallas as pl
from jax.experimental.pallas import tpu as pltpu

F32 = jnp.float32
BF16 = jnp.bfloat16
EPS = 1e-6
NEG_INF = -1e30

GRID_W = 64
S5_GROUP = 16
S5_STATE = 64
S5_T = 16
S5_PAIR = 2 * S5_GROUP
S5_BLOCK = 16
LANES = 128
TOK_TILE = 64
FF_CHUNK = 1024
NA_HEAD_DIM = 64
NA_WIN_R = 8
NA_WIN_C = 16
NA_QROWS = 4
NA_KROWS = 12
N_MOD = 6

V7X_VMEM_LIMIT_BYTES = 56 * 1024 * 1024
HIGHEST = lax.Precision.HIGHEST


def _params(**kw):
    return pltpu.CompilerParams(vmem_limit_bytes=V7X_VMEM_LIMIT_BYTES, **kw)


def _resident(shape):
    nd = len(shape)
    return pl.BlockSpec(shape, lambda *_: (0,) * nd, pipeline_mode=pl.Buffered(1))


def _layer_spec(stacked, idx):
    nd = stacked.ndim
    return pl.BlockSpec((None,) + stacked.shape[1:], lambda *_: (idx,) + (0,) * (nd - 1),
                        pipeline_mode=pl.Buffered(1))


def _rms(x, g):
    ms = jnp.mean(x * x, axis=-1, keepdims=True)
    return x * lax.rsqrt(ms + EPS) * g


def _sigmoid(x):
    return 1.0 / (1.0 + jnp.exp(-x))


def _gelu_tanh(x):
    c = math.sqrt(2.0 / math.pi)
    return x * (0.5 * (1.0 + jnp.tanh(c * (x + 0.044715 * (x * x * x)))))


def _mod_kernel(a_ref, w_ref, b_ref, o_ref):
    a = a_ref[...]
    act = a * _sigmoid(a)
    o_ref[0] = jnp.dot(act.astype(BF16), w_ref[0].astype(BF16), preferred_element_type=F32) + b_ref[0]


def _modulation(a, w_mod, b_mod):
    depth, d, n = w_mod.shape
    rows = a.shape[0]
    tn = 1536
    return pl.pallas_call(
        _mod_kernel,
        out_shape=jax.ShapeDtypeStruct((depth, rows, n), F32),
        grid=(depth, n // tn),
        in_specs=[
            pl.BlockSpec((rows, d), lambda l, j: (0, 0)),
            pl.BlockSpec((1, d, tn), lambda l, j: (l, 0, j)),
            pl.BlockSpec((1, 1, tn), lambda l, j: (l, 0, j)),
        ],
        out_specs=pl.BlockSpec((1, rows, tn), lambda l, j: (l, 0, j)),
        compiler_params=_params(),
        name="adaln_mod",
    )(a, w_mod, b_mod.reshape(depth, 1, n))


def _lane_quarter(shape):
    return lax.broadcasted_iota(jnp.int32, shape, 1) // S5_PAIR


def _tokens_to_pairs(tok_ref, pair_ref, bsz):
    per = LANES // S5_PAIR
    tt = tok_ref.shape[1] // bsz
    n_chunks = tt // S5_T
    quarter = _lane_quarter((n_chunks * bsz, LANES))
    for blk in range(tok_ref.shape[0]):
        for m in range(S5_T // per):
            src = [jnp.concatenate([tok_ref[blk, pl.ds(c * S5_T + per * m + jj, bsz, stride=tt), :]
                                    for c in range(n_chunks)], axis=0) for jj in range(per)]
            for qq in range(per):
                dest = None
                for jj in range(per):
                    shift = ((jj - qq) % per) * S5_PAIR
                    r = src[jj] if shift == 0 else pltpu.roll(src[jj], shift, axis=1)
                    dest = r if dest is None else jnp.where(quarter == jj, r, dest)
                pair_ref[per * blk + qq, :, m * LANES:(m + 1) * LANES] = dest


def _pairs_to_tokens(pair_ref, tok_ref, bsz):
    per = LANES // S5_PAIR
    tt = tok_ref.shape[1] // bsz
    n_chunks = tt // S5_T
    quarter = _lane_quarter((n_chunks * bsz, LANES))
    for blk in range(tok_ref.shape[0]):
        for m in range(S5_T // per):
            src = [pair_ref[per * blk + qq, :, m * LANES:(m + 1) * LANES] for qq in range(per)]
            for ts in range(per):
                dest = None
                for qq in range(per):
                    shift = ((qq - ts) % per) * S5_PAIR
                    r = src[qq] if shift == 0 else pltpu.roll(src[qq], shift, axis=1)
                    dest = r if dest is None else jnp.where(quarter == qq, r, dest)
                for c in range(n_chunks):
                    tok_ref[blk, pl.ds(c * S5_T + per * m + ts, bsz, stride=tt), :] = dest[c * bsz:(c + 1) * bsz]


def _modulated(h_ref, mod_ref, gain, shift_row, scale_row):
    parts = []
    for b in range(h_ref.shape[0]):
        m = mod_ref[b]
        y = _rms(h_ref[b], gain) * (1.0 + m[scale_row:scale_row + 1]) + m[shift_row:shift_row + 1]
        parts.append(y.astype(BF16))
    return jnp.concatenate(parts, axis=0)


def _pre_even_kernel(h_ref, mod_ref, g_ref, w_ref, up_ref, qkv_ref, us_ref, *, s5_width):
    bsz, tt, _ = h_ref.shape
    hl = _modulated(h_ref, mod_ref, g_ref[0:1], 0, 1)
    z = jnp.dot(hl, w_ref[...], preferred_element_type=F32)
    for blk in range(s5_width // LANES):
        us_ref[blk] = z[:, blk * LANES:(blk + 1) * LANES]
    for b in range(bsz):
        qkv_ref[b] = z[b * tt:(b + 1) * tt, s5_width:].astype(BF16)
    _tokens_to_pairs(us_ref, up_ref, bsz)


def _token_grid(seq):
    return (seq // TOK_TILE,)


def _tok_spec(bsz, width):
    return pl.BlockSpec((bsz, TOK_TILE, width), lambda i: (0, i, 0))


def _pre_even(h, mod, gains, w_in, layer, e):
    bsz, seq, d = h.shape
    n = w_in.shape[-1]
    s5w = d // 2
    npair = s5w // S5_PAIR
    pw = S5_T * S5_PAIR
    prow = TOK_TILE // S5_T * bsz
    return pl.pallas_call(
        functools.partial(_pre_even_kernel, s5_width=s5w),
        out_shape=(jax.ShapeDtypeStruct((npair, seq // S5_T * bsz, pw), F32),
                   jax.ShapeDtypeStruct((bsz, seq, n - s5w), BF16)),
        grid=_token_grid(seq),
        in_specs=[_tok_spec(bsz, d), _layer_spec(mod, layer), _layer_spec(gains, layer), _layer_spec(w_in, e)],
        out_specs=(pl.BlockSpec((npair, prow, pw), lambda i: (0, i, 0)), _tok_spec(bsz, n - s5w)),
        scratch_shapes=[pltpu.VMEM((s5w // LANES, bsz * TOK_TILE, LANES), F32)],
        compiler_params=_params(),
        name="pre_even",
    )(h, mod, gains, w_in)


def _pre_odd_kernel(h_ref, mod_ref, g_ref, w_ref, a_ref, b_ref, *, half):
    bsz, tt, _ = h_ref.shape
    hl = _modulated(h_ref, mod_ref, g_ref[0:1], 0, 1)
    z = jnp.dot(hl, w_ref[...], preferred_element_type=F32)
    lane = lax.broadcasted_iota(jnp.int32, (tt, LANES), 1)
    nyq = jnp.zeros((tt, LANES), F32)
    for b in range(bsz):
        zb = z[b * tt:(b + 1) * tt]
        a_ref[:, b * half:(b + 1) * half] = zb[:, :half].astype(BF16)
        b_ref[:, b * half:(b + 1) * half] = zb[:, half:2 * half].astype(BF16)
        nyq = jnp.where(lane == b, zb[:, 2 * half:], nyq)
    a_ref[:, bsz * half:] = nyq.astype(BF16)


def _pre_odd(h, mod, gains, w_cs, layer):
    bsz, seq, d = h.shape
    half = d // 2
    return pl.pallas_call(
        functools.partial(_pre_odd_kernel, half=half),
        out_shape=(jax.ShapeDtypeStruct((seq, bsz * half + LANES), BF16),
                   jax.ShapeDtypeStruct((seq, bsz * half), BF16)),
        grid=_token_grid(seq),
        in_specs=[_tok_spec(bsz, d), _layer_spec(mod, layer), _layer_spec(gains, layer), _resident(w_cs.shape)],
        out_specs=(pl.BlockSpec((TOK_TILE, bsz * half + LANES), lambda i: (i, 0)),
                   pl.BlockSpec((TOK_TILE, bsz * half), lambda i: (i, 0))),
        compiler_params=_params(),
        name="pre_odd",
    )(h, mod, gains, w_cs)


def _matmul_kernel(a_ref, b_ref, o_ref, acc_ref, *, nk):
    k = pl.program_id(2)

    @pl.when(k == 0)
    def _():
        acc_ref[...] = jnp.zeros_like(acc_ref)

    acc_ref[...] += jnp.dot(a_ref[...], b_ref[...], preferred_element_type=F32)

    @pl.when(k == nk - 1)
    def _():
        o_ref[...] = acc_ref[...].astype(o_ref.dtype)


def _matmul(a, b, tm, tn, tk, out_dtype, a_col0=0):
    m = a.shape[0]
    kdim, n = b.shape
    nk = kdim // tk
    return pl.pallas_call(
        functools.partial(_matmul_kernel, nk=nk),
        out_shape=jax.ShapeDtypeStruct((m, n), out_dtype),
        grid=(m // tm, n // tn, nk),
        in_specs=[pl.BlockSpec((tm, tk), lambda i, j, k: (i, k + a_col0)),
                  pl.BlockSpec((tk, tn), lambda i, j, k: (k, j))],
        out_specs=pl.BlockSpec((tm, tn), lambda i, j, k: (i, j)),
        scratch_shapes=[pltpu.VMEM((tm, tn), F32)],
        compiler_params=_params(),
        name="fnet_seq_dft",
    )(a, b)


def _dft_pair(n, right):
    n0 = LANES
    n1 = n // n0
    k = jnp.arange(n, dtype=jnp.int32)[:, None]
    pa = (k * (jnp.arange(n1, dtype=jnp.int32)[None, :] * n0)) % n
    pb = (k * jnp.arange(n0, dtype=jnp.int32)[None, :]) % n
    w = 2.0 * math.pi / n
    scale = 1.0 / math.sqrt(n)
    sign = 1.0 if right else -1.0
    ca, sa = jnp.cos(pa.astype(F32) * w) * scale, jnp.sin(pa.astype(F32) * w) * scale
    cb, sb = jnp.cos(pb.astype(F32) * w), jnp.sin(pb.astype(F32) * w)
    a1 = jnp.concatenate([ca, sign * sa], axis=1)
    a2 = jnp.concatenate([sa, -sign * ca], axis=1)
    tr = min(n, 256)
    rows = lambda w_: pl.BlockSpec((tr, w_), lambda i: (i, 0))
    return pl.pallas_call(
        _dft_expand_kernel,
        out_shape=jax.ShapeDtypeStruct((n, 2 * n), BF16),
        grid=(n // tr,),
        in_specs=[rows(2 * n1), rows(2 * n1), rows(n0), rows(n0)],
        out_specs=rows(2 * n),
        compiler_params=_params(),
        name="dft_expand",
    )(a1, a2, cb, sb)


def _dft_expand_kernel(a1_ref, a2_ref, cb_ref, sb_ref, o_ref):
    cb, sb = cb_ref[...], sb_ref[...]
    for j in range(a1_ref.shape[1]):
        blk = a1_ref[:, j:j + 1] * cb - a2_ref[:, j:j + 1] * sb
        o_ref[:, j * LANES:(j + 1) * LANES] = blk.astype(BF16)


def _s5_prepare(lam_re, lam_im, log_dt, b_re, b_im, c_re, c_im, d_skip):
    t_len, hh, pp = S5_T, S5_GROUP, S5_STATE
    ne, _, gg, _ = lam_re.shape
    lam_re = jnp.minimum(lam_re.astype(F32), -1e-4)
    lam_im = lam_im.astype(F32)
    dt = jnp.exp(log_dt.astype(F32))[..., None]
    mag = jnp.exp(lam_re * dt)
    a_re = mag * jnp.cos(lam_im * dt)
    a_im = mag * jnp.sin(lam_im * dt)
    den = lam_re * lam_re + lam_im * lam_im
    num_re = a_re - 1.0
    f_re = (num_re * lam_re + a_im * lam_im) / den
    f_im = (a_im * lam_re - num_re * lam_im) / den
    b_re = b_re.astype(F32)
    b_im = b_im.astype(F32)
    bb_re = f_re[..., None] * b_re - f_im[..., None] * b_im
    bb_im = f_re[..., None] * b_im + f_im[..., None] * b_re
    tau = jnp.arange(t_len + 1, dtype=F32).reshape(1, 1, 1, -1, 1)
    lr, li, dtt = lam_re[..., None, :], lam_im[..., None, :], dt[..., None]
    pmag = jnp.exp(lr * dtt * tau)
    p_re = pmag * jnp.cos(li * dtt * tau)
    p_im = pmag * jnp.sin(li * dtt * tau)
    bbt_re = jnp.swapaxes(bb_re, -1, -2)[:, :, :, None]
    bbt_im = jnp.swapaxes(bb_im, -1, -2)[:, :, :, None]
    abt_re = p_re[..., None, :] * bbt_re - p_im[..., None, :] * bbt_im
    abt_im = p_re[..., None, :] * bbt_im + p_im[..., None, :] * bbt_re
    c_re = c_re.astype(F32)[:, :, :, None]
    c_im = c_im.astype(F32)[:, :, :, None]
    e_re = c_re * p_re[..., None, :] - c_im * p_im[..., None, :]
    e_im = c_re * p_im[..., None, :] + c_im * p_re[..., None, :]

    gp = gg // 2
    pw = t_len * 2 * hh

    def cat(re, im):
        tn = re.shape[2]
        re = re.reshape(ne, gp, 2, tn, hh, pp)
        im = im.reshape(ne, gp, 2, tn, hh, pp)
        z = jnp.zeros((ne, gp, tn, hh, pp), F32)
        e0 = jnp.concatenate([re[:, :, 0], z, im[:, :, 0], z], axis=-1)
        e1 = jnp.concatenate([z, re[:, :, 1], z, im[:, :, 1]], axis=-1)
        return jnp.stack([e0, e1], axis=3).reshape(ne, gp, tn * 2 * hh, 4 * pp)

    fwd_t = slice(0, t_len)
    rev_t = slice(t_len - 1, None, -1)
    st_cols, out_rows, resp = [], [], []
    for d in range(2):
        tau_k = fwd_t if d == 0 else rev_t
        bcat0 = cat(abt_re[:, d, :, 0:1], abt_im[:, d, :, 0:1])
        ecat = cat(e_re[:, d, :, tau_k], -e_im[:, d, :, tau_k])
        resp.append(jnp.einsum('xqrk,xqck->xqrc', bcat0, ecat, precision=HIGHEST))
        tau_s = rev_t if d == 0 else fwd_t
        st_cols.append(cat(abt_re[:, d, :, tau_s], abt_im[:, d, :, tau_s]))
        tau_o = slice(1, t_len + 1) if d == 0 else slice(t_len, 0, -1)
        out_rows.append(jnp.swapaxes(cat(e_re[:, d, :, tau_o], -e_im[:, d, :, tau_o]), -1, -2))

    span = pw - 2 * hh
    kf = jnp.pad(resp[0], ((0, 0), (0, 0), (0, 0), (span, 0)))
    kb = jnp.pad(resp[1], ((0, 0), (0, 0), (0, 0), (0, span)))
    blocks = []
    for j in range(t_len):
        lo_f = span - 2 * hh * j
        lo_b = 2 * hh * (t_len - 1 - j)
        blocks.append(kf[..., lo_f:lo_f + pw] + kb[..., lo_b:lo_b + pw])
    toep = jnp.stack(blocks, axis=2).reshape(ne, gp, pw, pw)
    wbig = jnp.concatenate([toep] + st_cols, axis=-1).astype(BF16)
    wout = jnp.concatenate(out_rows, axis=2).astype(BF16)

    dec_rows = [p_re[:, 0, :, t_len], p_im[:, 0, :, t_len], p_re[:, 1, :, t_len], p_im[:, 1, :, t_len]]
    dec = jnp.stack([x.reshape(ne, gp, 2 * pp) for x in dec_rows], axis=2)
    dec = jnp.pad(dec, ((0, 0), (0, 0), (0, 4), (0, 0)))
    dsk = jnp.broadcast_to(d_skip.astype(F32).reshape(ne, gp, 1, 2 * hh), (ne, gp, t_len, 2 * hh))
    dsk = jnp.pad(dsk.reshape(ne, gp, 1, pw), ((0, 0), (0, 0), (0, 7), (0, 0)))
    return wbig, wout, dec, dsk


def _s5_kernel(uc_ref, ul_ref, wbig_ref, wout_ref, dec_ref, dsk_ref, yc_ref, yl_ref, z_ref, sin_ref, car_ref,
               *, bsz):
    tr = z_ref.shape[1]
    nb_c, nb_l = uc_ref.shape[1] // tr, ul_ref.shape[1] // tr
    nb = nb_c + nb_l
    wy = S5_T * S5_PAIR
    sw = 2 * S5_STATE
    parts = ((uc_ref, yc_ref, 0, nb_c), (ul_ref, yl_ref, nb_c, nb_l))

    for u_ref, _, base, nblk in parts:
        def stage1(i, carry, u_ref=u_ref, base=base):
            r = pl.multiple_of(i * tr, tr)
            ub = u_ref[0, pl.ds(r, tr), :].astype(BF16)
            z_ref[base + i] = jnp.dot(ub, wbig_ref[0], preferred_element_type=F32)
            return carry

        lax.fori_loop(0, nblk, stage1, 0)

    dec = dec_ref[0]
    for d in range(2):
        a_re = dec[2 * d:2 * d + 1][None]
        a_im = dec[2 * d + 1:2 * d + 2][None]
        lc = wy + 2 * d * sw
        sc = 2 * d * sw
        powers = [(jnp.ones_like(a_re), jnp.zeros_like(a_re))]
        for _ in range(S5_BLOCK):
            p_re, p_im = powers[-1]
            powers.append((p_re * a_re - p_im * a_im, p_re * a_im + p_im * a_re))
        steps = list(range(S5_BLOCK)) if d == 0 else list(range(S5_BLOCK - 1, -1, -1))
        if d == 0:
            block_order = list(range(nb))
        else:
            block_order = list(range(nb_c - 1, -1, -1)) + list(range(nb - 1, nb_c - 1, -1))

        s_re = jnp.zeros((nb, bsz, sw), F32)
        s_im = jnp.zeros((nb, bsz, sw), F32)
        for i in steps:
            rs = slice(i * bsz, (i + 1) * bsz)
            sin_ref[:, rs, sc:sc + sw] = s_re
            sin_ref[:, rs, sc + sw:sc + 2 * sw] = s_im
            l_re = z_ref[:, rs, lc:lc + sw]
            l_im = z_ref[:, rs, lc + sw:lc + 2 * sw]
            s_re, s_im = a_re * s_re - a_im * s_im + l_re, a_re * s_im + a_im * s_re + l_im

        g_re, g_im = powers[S5_BLOCK]
        c_re = jnp.zeros((bsz, sw), F32)
        c_im = jnp.zeros((bsz, sw), F32)
        for blk in block_order:
            car_ref[blk, :, 0:sw] = c_re
            car_ref[blk, :, sw:2 * sw] = c_im
            c_re, c_im = (g_re[0] * c_re - g_im[0] * c_im + s_re[blk], g_re[0] * c_im + g_im[0] * c_re + s_im[blk])

        e_re = car_ref[:, :, 0:sw]
        e_im = car_ref[:, :, sw:2 * sw]
        for n, i in enumerate(steps):
            p_re, p_im = powers[n]
            rs = slice(i * bsz, (i + 1) * bsz)
            sin_ref[:, rs, sc:sc + sw] = sin_ref[:, rs, sc:sc + sw] + (p_re * e_re - p_im * e_im)
            sin_ref[:, rs, sc + sw:sc + 2 * sw] = sin_ref[:, rs, sc + sw:sc + 2 * sw] + (p_re * e_im + p_im * e_re)

    dsk = dsk_ref[0][0:1]

    for u_ref, y_ref, base, nblk in parts:
        def stage3(i, carry, u_ref=u_ref, y_ref=y_ref, base=base):
            r = pl.multiple_of(i * tr, tr)
            sb = sin_ref[base + i].astype(BF16)
            y = (dsk * u_ref[0, pl.ds(r, tr), :] + z_ref[base + i, :, 0:wy]
                 + jnp.dot(sb, wout_ref[0], preferred_element_type=F32))
            y_ref[0, pl.ds(r, tr), :] = y
            return carry

        lax.fori_loop(0, nblk, stage3, 0)


def _s5_chunked(u_ctx, u_lat, wbig, wout, dec, dsk, e, bsz):
    gp, rows_c, width = u_ctx.shape
    rows_l = u_lat.shape[1]
    zc = wbig.shape[-1]
    tr = S5_BLOCK * bsz
    nb = (rows_c + rows_l) // tr
    per_pair = lambda shape: pl.BlockSpec((1,) + shape[1:], lambda q: (q, 0, 0))
    of_layer = lambda w: pl.BlockSpec((None, 1) + w.shape[2:], lambda q: (e, q, 0, 0))
    return pl.pallas_call(
        functools.partial(_s5_kernel, bsz=bsz),
        out_shape=(jax.ShapeDtypeStruct(u_ctx.shape, F32), jax.ShapeDtypeStruct(u_lat.shape, F32)),
        grid=(gp,),
        in_specs=[per_pair(u_ctx.shape), per_pair(u_lat.shape), of_layer(wbig), of_layer(wout),
                  of_layer(dec), of_layer(dsk)],
        out_specs=(per_pair(u_ctx.shape), per_pair(u_lat.shape)),
        scratch_shapes=[pltpu.VMEM((nb, tr, zc), F32), pltpu.VMEM((nb, tr, width), F32),
                        pltpu.VMEM((nb, bsz, 4 * S5_STATE), F32)],
        compiler_params=_params(),
        name="s5_chunked",
    )(u_ctx, u_lat, wbig, wout, dec, dsk)


def _attend_pairs(q_ref, k_refs, v_refs, bias_ref, o_ref, n_biased):
    width = q_ref.shape[-1]
    kb = k_refs[0].shape[1]
    lane = lax.broadcasted_iota(jnp.int32, (1, 2 * NA_HEAD_DIM), 1)
    outs = []
    for hp in range(width // (2 * NA_HEAD_DIM)):
        cs = slice(hp * 2 * NA_HEAD_DIM, (hp + 1) * 2 * NA_HEAD_DIM)
        qp = q_ref[0, :, cs]
        ks = [r[0, :, cs] for r in k_refs]
        vs = [r[0, :, cs] for r in v_refs]
        res = []
        for e in range(2):
            sel = (lane // NA_HEAD_DIM) == e
            qe = jnp.where(sel, qp, jnp.zeros_like(qp))
            parts = []
            for i, kk in enumerate(ks):
                s = lax.dot_general(qe, kk, (((1,), (1,)), ((), ())), preferred_element_type=F32)
                if i < n_biased:
                    s = s + bias_ref[0, 2 * hp + e, :, i * kb:(i + 1) * kb]
                parts.append(s)
            m = parts[0].max(axis=-1, keepdims=True)
            for s in parts[1:]:
                m = jnp.maximum(m, s.max(axis=-1, keepdims=True))
            ps = [jnp.exp(s - m) for s in parts]
            den = ps[0].sum(axis=-1, keepdims=True)
            for p in ps[1:]:
                den = den + p.sum(axis=-1, keepdims=True)
            acc = jnp.dot(ps[0].astype(BF16), vs[0], preferred_element_type=F32)
            for p, vv in zip(ps[1:], vs[1:]):
                acc = acc + jnp.dot(p.astype(BF16), vv, preferred_element_type=F32)
            res.append(acc / den)
        outs.append(jnp.where(lane < NA_HEAD_DIM, res[0], res[1]))
    o_ref[0] = jnp.concatenate(outs, axis=-1).astype(o_ref.dtype)


def _na_lat_kernel(q_ref, k0, k1, k2, v0, v1, v2, kc, vc, bias_ref, o_ref):
    _attend_pairs(q_ref, (k0, k1, k2, kc), (v0, v1, v2, vc), bias_ref, o_ref, 3)


def _na_ctx_kernel(q_ref, kc, vc, o_ref):
    _attend_pairs(q_ref, (kc,), (vc,), None, o_ref, 0)


def _na_bias_table(rpb, rows):
    nh = rpb.shape[0]
    nblk = rows // NA_QROWS
    blocks = np.array([0, 1, nblk - 1])
    kb0 = np.clip(blocks - 1, 0, nblk - NA_KROWS // NA_QROWS) * NA_QROWS
    qr = blocks[:, None] * NA_QROWS + np.arange(NA_QROWS)[None, :]
    r0 = np.clip(qr - NA_WIN_R // 2, 0, rows - NA_WIN_R)
    kr = kb0[:, None] + np.arange(NA_KROWS)[None, :]
    row_ok = (kr[:, None, :] >= r0[:, :, None]) & (kr[:, None, :] < r0[:, :, None] + NA_WIN_R)
    n_dr = 2 * NA_WIN_R - 1
    dr = kr[:, None, :] - qr[:, :, None] + (NA_WIN_R - 1)
    cols = np.arange(GRID_W)
    c0 = np.clip(cols - NA_WIN_C // 2, 0, GRID_W - NA_WIN_C)
    col_ok = (cols[None, :] >= c0[:, None]) & (cols[None, :] < c0[:, None] + NA_WIN_C)
    dc = np.clip(cols[None, :] - cols[:, None], -(NA_WIN_C - 1), NA_WIN_C - 1) + (NA_WIN_C - 1)
    sel_c = (np.arange(2 * NA_WIN_C - 1)[:, None, None] == dc[None]).astype(np.float32)
    by_col = jnp.einsum('hrc,cqk->hrqk', rpb.astype(F32), sel_c, precision=HIGHEST)
    zero = jnp.zeros((nh, 1, GRID_W, GRID_W), F32)
    padded = jnp.concatenate([zero, by_col, zero], axis=1)
    two_rows = jnp.concatenate([padded[:, :-1], padded[:, 1:]], axis=-1)
    variants = []
    for v in range(len(blocks)):
        strips = []
        for i in range(NA_QROWS):
            first = np.clip(dr[v, i, 0::2], -1, n_dr - 1) + 1
            strips.append(jnp.concatenate([two_rows[:, int(s)] for s in first], axis=-1))
        variants.append(jnp.concatenate(strips, axis=1))
    bias = jnp.stack(variants, axis=0)
    ok = (row_ok[:, :, None, :, None] & col_ok[None, None, :, None, :]).reshape(
        len(blocks), 1, NA_QROWS * GRID_W, NA_KROWS * GRID_W)
    return jnp.where(ok, bias, NEG_INF)


def _na_latent(qkv, qkv_c, bias, e):
    bsz, seq, w3 = qkv.shape
    width = w3 // 3
    nh = width // NA_HEAD_DIM
    lc = qkv_c.shape[1]
    qb = NA_QROWS * GRID_W
    nblk = seq // qb
    nkb = NA_KROWS // NA_QROWS

    def kv_spec(col, s):
        return pl.BlockSpec((1, qb, width),
                            lambda b, a: (b, jnp.clip(a - 1, 0, nblk - nkb) + s, col))

    def variant(a):
        return (a > 0).astype(jnp.int32) + (a == nblk - 1).astype(jnp.int32)

    return pl.pallas_call(
        _na_lat_kernel,
        out_shape=jax.ShapeDtypeStruct((bsz, seq, width), BF16),
        grid=(bsz, nblk),
        in_specs=[pl.BlockSpec((1, qb, width), lambda b, a: (b, a, 0))]
        + [kv_spec(1, s) for s in range(nkb)] + [kv_spec(2, s) for s in range(nkb)]
        + [pl.BlockSpec((1, lc, width), lambda b, a: (b, 0, 1)),
           pl.BlockSpec((1, lc, width), lambda b, a: (b, 0, 2)),
           pl.BlockSpec((1, nh) + bias.shape[2:], lambda b, a: (variant(a), e, 0, 0))],
        out_specs=pl.BlockSpec((1, qb, width), lambda b, a: (b, a, 0)),
        compiler_params=_params(),
        name="na_latent",
    )(qkv, *([qkv] * (2 * nkb)), qkv_c, qkv_c, bias)


def _na_context(qkv_c):
    bsz, lc, w3 = qkv_c.shape
    width = w3 // 3
    return pl.pallas_call(
        _na_ctx_kernel,
        out_shape=jax.ShapeDtypeStruct((bsz, lc, width), BF16),
        grid=(bsz,),
        in_specs=[pl.BlockSpec((1, lc, width), lambda b: (b, 0, 0)),
                  pl.BlockSpec((1, lc, width), lambda b: (b, 0, 1)),
                  pl.BlockSpec((1, lc, width), lambda b: (b, 0, 2))],
        out_specs=pl.BlockSpec((1, lc, width), lambda b: (b, 0, 0)),
        compiler_params=_params(),
        name="na_context",
    )(qkv_c, qkv_c, qkv_c)


def _residual_mlp(out_l, h_ref, mod_ref, gains, w1_ref, w2_ref, o_ref):
    bsz, tt, _ = h_ref.shape
    parts = []
    for b in range(bsz):
        m = mod_ref[b]
        h1 = h_ref[b] + m[2:3] * _rms(out_l[b * tt:(b + 1) * tt], gains[1:2])
        o_ref[b] = h1
        parts.append((_rms(h1, gains[2:3]) * (1.0 + m[4:5]) + m[3:4]).astype(BF16))
    hf = jnp.concatenate(parts, axis=0)
    d_ff = w1_ref.shape[1]
    acc = None
    for kf in range(d_ff // FF_CHUNK):
        cs = slice(kf * FF_CHUNK, (kf + 1) * FF_CHUNK)
        hid = jnp.maximum(jnp.dot(hf, w1_ref[:, cs], preferred_element_type=F32), 0.0)
        part = jnp.dot((hid * hid).astype(BF16), w2_ref[cs, :], preferred_element_type=F32)
        acc = part if acc is None else acc + part
    for b in range(bsz):
        o_ref[b] = o_ref[b] + mod_ref[b][5:6] * _rms(acc[b * tt:(b + 1) * tt], gains[3:4])


def _post_even_kernel(yp_ref, na_ref, h_ref, mod_ref, g_ref, wglu_ref, wo_ref, w1_ref, w2_ref, o_ref, ys_ref):
    bsz, tt, na_w = na_ref.shape
    _pairs_to_tokens(yp_ref, ys_ref, bsz)
    g = _gelu_tanh(jnp.concatenate([ys_ref[blk] for blk in range(ys_ref.shape[0])], axis=-1))
    gate = _sigmoid(jnp.dot(g.astype(BF16), wglu_ref[...], preferred_element_type=F32))
    s5 = (g * gate).astype(BF16)
    sw = s5.shape[1]
    out_l = (jnp.dot(s5, wo_ref[0:sw, :], preferred_element_type=F32)
             + jnp.dot(na_ref[...].reshape(bsz * tt, na_w), wo_ref[sw:, :], preferred_element_type=F32))
    _residual_mlp(out_l, h_ref, mod_ref, g_ref[...], w1_ref, w2_ref, o_ref)


def _post_odd_kernel(p_ref, q_ref, h_ref, mod_ref, g_ref, wp_ref, wq_ref, wn_ref, w1_ref, w2_ref, o_ref):
    bsz, tt, _ = h_ref.shape
    half = wp_ref.shape[0]
    pc = jnp.concatenate([p_ref[:, b * half:(b + 1) * half] for b in range(bsz)], axis=0)
    qc = jnp.concatenate([q_ref[:, b * half:(b + 1) * half] for b in range(bsz)], axis=0)
    out_l = (jnp.dot(pc, wp_ref[...], preferred_element_type=F32)
             + jnp.dot(qc, wq_ref[...], preferred_element_type=F32))
    lane = lax.broadcasted_iota(jnp.int32, (tt, LANES), 1)
    nyq_blk = p_ref[:, bsz * half:].astype(F32)
    nyq = jnp.concatenate([jnp.sum(jnp.where(lane == b, nyq_blk, 0.0), axis=1, keepdims=True)
                           for b in range(bsz)], axis=0)
    out_l = out_l + nyq * wn_ref[0:1, :]
    _residual_mlp(out_l, h_ref, mod_ref, g_ref[...], w1_ref, w2_ref, o_ref)


def _post_even(y_pairs, na, h, mod, gains, w_glu, w_out, w1, w2, layer, e):
    bsz, seq, d = h.shape
    npair = y_pairs.shape[0]
    pw = S5_T * S5_PAIR
    prow = TOK_TILE // S5_T * bsz
    return pl.pallas_call(
        _post_even_kernel,
        out_shape=jax.ShapeDtypeStruct(h.shape, F32),
        grid=_token_grid(seq),
        in_specs=[pl.BlockSpec((npair, prow, pw), lambda i: (0, i, 0)), _tok_spec(bsz, na.shape[-1]),
                  _tok_spec(bsz, d), _layer_spec(mod, layer), _layer_spec(gains, layer),
                  _layer_spec(w_glu, e), _layer_spec(w_out, e), _layer_spec(w1, layer), _layer_spec(w2, layer)],
        out_specs=_tok_spec(bsz, d),
        scratch_shapes=[pltpu.VMEM((npair * S5_PAIR // LANES, bsz * TOK_TILE, LANES), F32)],
        compiler_params=_params(),
        name="post_even",
    )(y_pairs, na, h, mod, gains, w_glu, w_out, w1, w2)


def _post_odd(p, q, h, mod, gains, w_p, w_q, w_n, w1, w2, layer, o):
    bsz, seq, d = h.shape
    rows = lambda arr: pl.BlockSpec((TOK_TILE, arr.shape[1]), lambda i: (i, 0))
    return pl.pallas_call(
        _post_odd_kernel,
        out_shape=jax.ShapeDtypeStruct(h.shape, F32),
        grid=_token_grid(seq),
        in_specs=[rows(p), rows(q), _tok_spec(bsz, d),
                  _layer_spec(mod, layer), _layer_spec(gains, layer),
                  _layer_spec(w_p, o), _layer_spec(w_q, o), _layer_spec(w_n, o),
                  _layer_spec(w1, layer), _layer_spec(w2, layer)],
        out_specs=_tok_spec(bsz, d),
        compiler_params=_params(),
        name="post_odd",
    )(p, q, h, mod, gains, w_p, w_q, w_n, w1, w2)


def _fnet_seq_dft(dft, a, b, tm, tk):
    seq = dft.shape[0]
    p = _matmul(dft, a, tm, a.shape[1] // 3, tk, BF16)
    q = _matmul(dft, b, tm, b.shape[1] // 2, tk, BF16, a_col0=seq // tk)
    return p, q


def kernel(x, c, ctx, c_ctx, w_mod, b_mod, norm_g, w_in, w_out_even, s5_lam_re, s5_lam_im, s5_log_dt,
           s5_b_re, s5_b_im, s5_c_re, s5_c_im, s5_d, s5_w_glu, na_rpb, w_fourier, w_ff1, w_ff2):
    bsz, seq, d = x.shape
    lc = ctx.shape[1]
    depth = w_mod.shape[0]
    s5w = s5_d.shape[-1]
    last_ctx_layer = 2 * ((depth - 1) // 2)

    mod_rows = 16
    a = jnp.concatenate([c, c_ctx[None, :], jnp.zeros((mod_rows - bsz - 1, d), F32)], axis=0)
    mod = _modulation(a, w_mod, b_mod)
    pad_mod = lambda m: jnp.pad(m, ((0, 0), (0, 0), (0, 8 - N_MOD), (0, 0)))
    mod_l = pad_mod(mod[:, :bsz].reshape(depth, bsz, N_MOD, d))
    mod_c = pad_mod(jnp.broadcast_to(mod[:, bsz:bsz + 1].reshape(depth, 1, N_MOD, d), (depth, bsz, N_MOD, d)))
    gains = jnp.pad(norm_g.astype(F32), ((0, 0), (0, 4), (0, 0)))

    na_w = (w_in.shape[-1] - s5w) // 3
    qscale = jnp.concatenate([jnp.ones((s5w,), F32), jnp.full((na_w,), NA_HEAD_DIM ** -0.5, F32),
                              jnp.ones((2 * na_w,), F32)])
    w_in_b = (w_in * qscale).astype(BF16)
    w_out_b = w_out_even.astype(BF16)
    w_glu_b = s5_w_glu.astype(BF16)
    half = d // 2
    wf_lo = w_fourier[:, 1:half]
    wf_hi = w_fourier[:, half + 1:][:, ::-1]
    w_p_b = jnp.concatenate([w_fourier[:, 0:1], wf_lo + wf_hi], axis=1).astype(BF16)
    w_q_b = jnp.concatenate([jnp.zeros_like(w_fourier[:, 0:1]), wf_lo - wf_hi], axis=1).astype(BF16)
    w_n = jnp.pad(w_fourier[:, half:half + 1].astype(F32), ((0, 0), (0, 7), (0, 0)))
    w1_b = w_ff1.astype(BF16)
    w2_b = w_ff2.astype(BF16)

    wbig, wout, dec, dsk = _s5_prepare(s5_lam_re, s5_lam_im, s5_log_dt, s5_b_re, s5_b_im,
                                       s5_c_re, s5_c_im, s5_d)
    dft_d = _dft_pair(d, right=True)
    dft_dh = jnp.concatenate([dft_d[:, :half], dft_d[:, d:d + half],
                              jnp.broadcast_to(dft_d[:, half:half + 1], (d, LANES))], axis=1)
    dft_l = _dft_pair(seq, right=False)
    dft_c = _dft_pair(lc, right=False)

    bias = _na_bias_table(na_rpb.reshape((-1,) + na_rpb.shape[2:]), seq // GRID_W)

    h, s = x, ctx
    for layer in range(depth):
        upd_ctx = layer < last_ctx_layer
        if layer % 2 == 0:
            e = layer // 2
            u_l, qkv_l = _pre_even(h, mod_l, gains, w_in_b, layer, e)
            u_c, qkv_c = _pre_even(s, mod_c, gains, w_in_b, layer, e)
            y_c, y_l = _s5_chunked(u_c, u_l, wbig, wout, dec, dsk, e, bsz)
            na_l = _na_latent(qkv_l, qkv_c, bias, e)
            h_new = _post_even(y_l, na_l, h, mod_l, gains, w_glu_b, w_out_b, w1_b, w2_b, layer, e)
            if upd_ctx:
                na_c = _na_context(qkv_c)
                s = _post_even(y_c, na_c, s, mod_c, gains, w_glu_b, w_out_b, w1_b, w2_b, layer, e)
            h = h_new
        else:
            o = layer // 2
            a_l, b_l = _pre_odd(h, mod_l, gains, dft_dh, layer)
            p_l, q_l = _fnet_seq_dft(dft_l, a_l, b_l, 1024, 1024)
            h_new = _post_odd(p_l, q_l, h, mod_l, gains, w_p_b, w_q_b, w_n, w1_b, w2_b, layer, o)
            if upd_ctx:
                a_c, b_c = _pre_odd(s, mod_c, gains, dft_dh, layer)
                p_c, q_c = _fnet_seq_dft(dft_c, a_c, b_c, lc, lc)
                s = _post_odd(p_c, q_c, s, mod_c, gains, w_p_b, w_q_b, w_n, w1_b, w2_b, layer, o)
            h = h_new
    return h
```

```python
import functools
import math

import jax
import jax.numpy as jnp
import numpy as np
from jax import lax
from jax.experimental import pallas as pl
from jax.experimental.pallas import tpu as pltpu

F32 = jnp.float32
BF16 = jnp.bfloat16
EPS = 1e-6
NEG_INF = -1e30

GRID_W = 64
S5_GROUP = 16
S5_STATE = 64
S5_T = 16
S5_PAIR = 2 * S5_GROUP
S5_BLOCK = 16
LANES = 128
TOK_TILE = 64
TOK_TILE_WIDE = 128
TOK_TILE_POST_ODD = 128
FF_CHUNK = 1024
NA_HEAD_DIM = 64
NA_WIN_R = 8
NA_WIN_C = 16
NA_QROWS = 4
NA_KROWS = 12
N_MOD = 6

V7X_VMEM_LIMIT_BYTES = 56 * 1024 * 1024
HIGHEST = lax.Precision.HIGHEST


def _params(**kw):
    return pltpu.CompilerParams(vmem_limit_bytes=V7X_VMEM_LIMIT_BYTES, **kw)


def _resident(shape):
    nd = len(shape)
    return pl.BlockSpec(shape, lambda *_: (0,) * nd, pipeline_mode=pl.Buffered(1))


def _layer_spec(stacked, idx):
    nd = stacked.ndim
    return pl.BlockSpec((None,) + stacked.shape[1:], lambda *_: (idx,) + (0,) * (nd - 1),
                        pipeline_mode=pl.Buffered(1))


def _rms(x, g):
    ms = jnp.mean(x * x, axis=-1, keepdims=True)
    return x * lax.rsqrt(ms + EPS) * g


def _sigmoid(x):
    return 1.0 / (1.0 + jnp.exp(-x))


def _gelu_tanh(x):
    c = math.sqrt(2.0 / math.pi)
    return x * (0.5 * (1.0 + jnp.tanh(c * (x + 0.044715 * (x * x * x)))))


def _mod_kernel(a_ref, w_ref, b_ref, o_ref):
    a = a_ref[...]
    act = a * _sigmoid(a)
    o_ref[0] = jnp.dot(act.astype(BF16), w_ref[0].astype(BF16), preferred_element_type=F32) + b_ref[0]


def _modulation(a, w_mod, b_mod):
    depth, d, n = w_mod.shape
    rows = a.shape[0]
    tn = 1536
    return pl.pallas_call(
        _mod_kernel,
        out_shape=jax.ShapeDtypeStruct((depth, rows, n), F32),
        grid=(depth, n // tn),
        in_specs=[
            pl.BlockSpec((rows, d), lambda l, j: (0, 0)),
            pl.BlockSpec((1, d, tn), lambda l, j: (l, 0, j)),
            pl.BlockSpec((1, 1, tn), lambda l, j: (l, 0, j)),
        ],
        out_specs=pl.BlockSpec((1, rows, tn), lambda l, j: (l, 0, j)),
        compiler_params=_params(),
        name="adaln_mod",
    )(a, w_mod, b_mod.reshape(depth, 1, n))


def _lane_quarter(shape):
    return lax.broadcasted_iota(jnp.int32, shape, 1) // S5_PAIR


def _tokens_to_pairs(tok_ref, pair_ref, bsz):
    per = LANES // S5_PAIR
    tt = tok_ref.shape[1] // bsz
    n_chunks = tt // S5_T
    quarter = _lane_quarter((n_chunks * bsz, LANES))
    for blk in range(tok_ref.shape[0]):
        for m in range(S5_T // per):
            src = [jnp.concatenate([tok_ref[blk, pl.ds(c * S5_T + per * m + jj, bsz, stride=tt), :]
                                    for c in range(n_chunks)], axis=0) for jj in range(per)]
            for qq in range(per):
                dest = None
                for jj in range(per):
                    shift = ((jj - qq) % per) * S5_PAIR
                    r = src[jj] if shift == 0 else pltpu.roll(src[jj], shift, axis=1)
                    dest = r if dest is None else jnp.where(quarter == jj, r, dest)
                pair_ref[per * blk + qq, :, m * LANES:(m + 1) * LANES] = dest


def _pairs_to_tokens(pair_ref, tok_ref, bsz):
    per = LANES // S5_PAIR
    tt = tok_ref.shape[1] // bsz
    n_chunks = tt // S5_T
    quarter = _lane_quarter((n_chunks * bsz, LANES))
    for blk in range(tok_ref.shape[0]):
        for m in range(S5_T // per):
            src = [pair_ref[per * blk + qq, :, m * LANES:(m + 1) * LANES] for qq in range(per)]
            for ts in range(per):
                dest = None
                for qq in range(per):
                    shift = ((qq - ts) % per) * S5_PAIR
                    r = src[qq] if shift == 0 else pltpu.roll(src[qq], shift, axis=1)
                    dest = r if dest is None else jnp.where(quarter == qq, r, dest)
                for c in range(n_chunks):
                    tok_ref[blk, pl.ds(c * S5_T + per * m + ts, bsz, stride=tt), :] = dest[c * bsz:(c + 1) * bsz]


def _modulated(h_ref, mod_ref, gain, shift_row, scale_row):
    parts = []
    for b in range(h_ref.shape[0]):
        m = mod_ref[b]
        y = _rms(h_ref[b], gain) * (1.0 + m[scale_row:scale_row + 1]) + m[shift_row:shift_row + 1]
        parts.append(y.astype(BF16))
    return jnp.concatenate(parts, axis=0)


def _pre_even_kernel(h_ref, mod_ref, g_ref, w_ref, up_ref, qkv_ref, us_ref, *, s5_width):
    bsz, tt, _ = h_ref.shape
    hl = _modulated(h_ref, mod_ref, g_ref[0:1], 0, 1)
    z = jnp.dot(hl, w_ref[...], preferred_element_type=F32)
    for blk in range(s5_width // LANES):
        us_ref[blk] = z[:, blk * LANES:(blk + 1) * LANES]
    for b in range(bsz):
        qkv_ref[b] = z[b * tt:(b + 1) * tt, s5_width:].astype(BF16)
    _tokens_to_pairs(us_ref, up_ref, bsz)


def _token_grid(seq, tile):
    return (seq // tile,)


def _tok_spec(bsz, width, tile):
    return pl.BlockSpec((bsz, tile, width), lambda i: (0, i, 0))


def _pre_even(h, mod, gains, w_in, layer, e):
    bsz, seq, d = h.shape
    n = w_in.shape[-1]
    s5w = d // 2
    npair = s5w // S5_PAIR
    pw = S5_T * S5_PAIR
    tile = TOK_TILE_WIDE
    prow = tile // S5_T * bsz
    return pl.pallas_call(
        functools.partial(_pre_even_kernel, s5_width=s5w),
        out_shape=(jax.ShapeDtypeStruct((npair, seq // S5_T * bsz, pw), F32),
                   jax.ShapeDtypeStruct((bsz, seq, n - s5w), BF16)),
        grid=_token_grid(seq, tile),
        in_specs=[_tok_spec(bsz, d, tile), _layer_spec(mod, layer), _layer_spec(gains, layer),
                  _layer_spec(w_in, e)],
        out_specs=(pl.BlockSpec((npair, prow, pw), lambda i: (0, i, 0)), _tok_spec(bsz, n - s5w, tile)),
        scratch_shapes=[pltpu.VMEM((s5w // LANES, bsz * tile, LANES), F32)],
        compiler_params=_params(),
        name="pre_even",
    )(h, mod, gains, w_in)


def _pre_odd_kernel(h_ref, mod_ref, g_ref, w_ref, a_ref, b_ref, *, half):
    bsz, tt, _ = h_ref.shape
    hl = _modulated(h_ref, mod_ref, g_ref[0:1], 0, 1)
    z = jnp.dot(hl, w_ref[...], preferred_element_type=F32)
    lane = lax.broadcasted_iota(jnp.int32, (tt, LANES), 1)
    nyq = jnp.zeros((tt, LANES), F32)
    for b in range(bsz):
        zb = z[b * tt:(b + 1) * tt]
        a_ref[:, b * half:(b + 1) * half] = zb[:, :half].astype(BF16)
        b_ref[:, b * half:(b + 1) * half] = zb[:, half:2 * half].astype(BF16)
        nyq = jnp.where(lane == b, zb[:, 2 * half:], nyq)
    a_ref[:, bsz * half:] = nyq.astype(BF16)


def _pre_odd(h, mod, gains, w_cs, layer):
    bsz, seq, d = h.shape
    half = d // 2
    return pl.pallas_call(
        functools.partial(_pre_odd_kernel, half=half),
        out_shape=(jax.ShapeDtypeStruct((seq, bsz * half + LANES), BF16),
                   jax.ShapeDtypeStruct((seq, bsz * half), BF16)),
        grid=_token_grid(seq, TOK_TILE_WIDE),
        in_specs=[_tok_spec(bsz, d, TOK_TILE_WIDE), _layer_spec(mod, layer), _layer_spec(gains, layer),
                  _resident(w_cs.shape)],
        out_specs=(pl.BlockSpec((TOK_TILE_WIDE, bsz * half + LANES), lambda i: (i, 0)),
                   pl.BlockSpec((TOK_TILE_WIDE, bsz * half), lambda i: (i, 0))),
        compiler_params=_params(),
        name="pre_odd",
    )(h, mod, gains, w_cs)


def _matmul_kernel(a_ref, b_ref, o_ref, acc_ref, *, nk):
    k = pl.program_id(2)

    @pl.when(k == 0)
    def _():
        acc_ref[...] = jnp.zeros_like(acc_ref)

    acc_ref[...] += jnp.dot(a_ref[...], b_ref[...], preferred_element_type=F32)

    @pl.when(k == nk - 1)
    def _():
        o_ref[...] = acc_ref[...].astype(o_ref.dtype)


def _matmul(a, b, tm, tn, tk, out_dtype, a_col0=0):
    m = a.shape[0]
    kdim, n = b.shape
    nk = kdim // tk
    return pl.pallas_call(
        functools.partial(_matmul_kernel, nk=nk),
        out_shape=jax.ShapeDtypeStruct((m, n), out_dtype),
        grid=(m // tm, n // tn, nk),
        in_specs=[pl.BlockSpec((tm, tk), lambda i, j, k: (i, k + a_col0)),
                  pl.BlockSpec((tk, tn), lambda i, j, k: (k, j))],
        out_specs=pl.BlockSpec((tm, tn), lambda i, j, k: (i, j)),
        scratch_shapes=[pltpu.VMEM((tm, tn), F32)],
        compiler_params=_params(),
        name="fnet_seq_dft",
    )(a, b)


def _dft_pair(n, right):
    n0 = LANES
    n1 = n // n0
    k = jnp.arange(n, dtype=jnp.int32)[:, None]
    pa = (k * (jnp.arange(n1, dtype=jnp.int32)[None, :] * n0)) % n
    pb = (k * jnp.arange(n0, dtype=jnp.int32)[None, :]) % n
    w = 2.0 * math.pi / n
    scale = 1.0 / math.sqrt(n)
    sign = 1.0 if right else -1.0
    ca, sa = jnp.cos(pa.astype(F32) * w) * scale, jnp.sin(pa.astype(F32) * w) * scale
    cb, sb = jnp.cos(pb.astype(F32) * w), jnp.sin(pb.astype(F32) * w)
    a1 = jnp.concatenate([ca, sign * sa], axis=1)
    a2 = jnp.concatenate([sa, -sign * ca], axis=1)
    tr = min(n, 256)
    rows = lambda w_: pl.BlockSpec((tr, w_), lambda i: (i, 0))
    return pl.pallas_call(
        _dft_expand_kernel,
        out_shape=jax.ShapeDtypeStruct((n, 2 * n), BF16),
        grid=(n // tr,),
        in_specs=[rows(2 * n1), rows(2 * n1), rows(n0), rows(n0)],
        out_specs=rows(2 * n),
        compiler_params=_params(),
        name="dft_expand",
    )(a1, a2, cb, sb)


def _dft_expand_kernel(a1_ref, a2_ref, cb_ref, sb_ref, o_ref):
    cb, sb = cb_ref[...], sb_ref[...]
    for j in range(a1_ref.shape[1]):
        blk = a1_ref[:, j:j + 1] * cb - a2_ref[:, j:j + 1] * sb
        o_ref[:, j * LANES:(j + 1) * LANES] = blk.astype(BF16)


def _s5_prepare(lam_re, lam_im, log_dt, b_re, b_im, c_re, c_im, d_skip):
    t_len, hh, pp = S5_T, S5_GROUP, S5_STATE
    ne, _, gg, _ = lam_re.shape
    lam_re = jnp.minimum(lam_re.astype(F32), -1e-4)
    lam_im = lam_im.astype(F32)
    dt = jnp.exp(log_dt.astype(F32))[..., None]
    mag = jnp.exp(lam_re * dt)
    a_re = mag * jnp.cos(lam_im * dt)
    a_im = mag * jnp.sin(lam_im * dt)
    den = lam_re * lam_re + lam_im * lam_im
    num_re = a_re - 1.0
    f_re = (num_re * lam_re + a_im * lam_im) / den
    f_im = (a_im * lam_re - num_re * lam_im) / den
    b_re = b_re.astype(F32)
    b_im = b_im.astype(F32)
    bb_re = f_re[..., None] * b_re - f_im[..., None] * b_im
    bb_im = f_re[..., None] * b_im + f_im[..., None] * b_re
    tau = jnp.arange(t_len + 1, dtype=F32).reshape(1, 1, 1, -1, 1)
    lr, li, dtt = lam_re[..., None, :], lam_im[..., None, :], dt[..., None]
    pmag = jnp.exp(lr * dtt * tau)
    p_re = pmag * jnp.cos(li * dtt * tau)
    p_im = pmag * jnp.sin(li * dtt * tau)
    bbt_re = jnp.swapaxes(bb_re, -1, -2)[:, :, :, None]
    bbt_im = jnp.swapaxes(bb_im, -1, -2)[:, :, :, None]
    abt_re = p_re[..., None, :] * bbt_re - p_im[..., None, :] * bbt_im
    abt_im = p_re[..., None, :] * bbt_im + p_im[..., None, :] * bbt_re
    c_re = c_re.astype(F32)[:, :, :, None]
    c_im = c_im.astype(F32)[:, :, :, None]
    e_re = c_re * p_re[..., None, :] - c_im * p_im[..., None, :]
    e_im = c_re * p_im[..., None, :] + c_im * p_re[..., None, :]

    gp = gg // 2
    pw = t_len * 2 * hh

    def cat(re, im):
        tn = re.shape[2]
        re = re.reshape(ne, gp, 2, tn, hh, pp)
        im = im.reshape(ne, gp, 2, tn, hh, pp)
        z = jnp.zeros((ne, gp, tn, hh, pp), F32)
        e0 = jnp.concatenate([re[:, :, 0], z, im[:, :, 0], z], axis=-1)
        e1 = jnp.concatenate([z, re[:, :, 1], z, im[:, :, 1]], axis=-1)
        return jnp.stack([e0, e1], axis=3).reshape(ne, gp, tn * 2 * hh, 4 * pp)

    fwd_t = slice(0, t_len)
    rev_t = slice(t_len - 1, None, -1)
    st_cols, out_rows, resp = [], [], []
    for d in range(2):
        tau_k = fwd_t if d == 0 else rev_t
        bcat0 = cat(abt_re[:, d, :, 0:1], abt_im[:, d, :, 0:1])
        ecat = cat(e_re[:, d, :, tau_k], -e_im[:, d, :, tau_k])
        resp.append(jnp.einsum('xqrk,xqck->xqrc', bcat0, ecat, precision=HIGHEST))
        tau_s = rev_t if d == 0 else fwd_t
        st_cols.append(cat(abt_re[:, d, :, tau_s], abt_im[:, d, :, tau_s]))
        tau_o = slice(1, t_len + 1) if d == 0 else slice(t_len, 0, -1)
        out_rows.append(jnp.swapaxes(cat(e_re[:, d, :, tau_o], -e_im[:, d, :, tau_o]), -1, -2))

    span = pw - 2 * hh
    kf = jnp.pad(resp[0], ((0, 0), (0, 0), (0, 0), (span, 0)))
    kb = jnp.pad(resp[1], ((0, 0), (0, 0), (0, 0), (0, span)))
    blocks = []
    for j in range(t_len):
        lo_f = span - 2 * hh * j
        lo_b = 2 * hh * (t_len - 1 - j)
        blocks.append(kf[..., lo_f:lo_f + pw] + kb[..., lo_b:lo_b + pw])
    toep = jnp.stack(blocks, axis=2).reshape(ne, gp, pw, pw)
    wbig = jnp.concatenate([toep] + st_cols, axis=-1).astype(BF16)
    wout = jnp.concatenate(out_rows, axis=2).astype(BF16)

    dec_rows = [p_re[:, 0, :, t_len], p_im[:, 0, :, t_len], p_re[:, 1, :, t_len], p_im[:, 1, :, t_len]]
    dec = jnp.stack([x.reshape(ne, gp, 2 * pp) for x in dec_rows], axis=2)
    dec = jnp.pad(dec, ((0, 0), (0, 0), (0, 4), (0, 0)))
    dsk = jnp.broadcast_to(d_skip.astype(F32).reshape(ne, gp, 1, 2 * hh), (ne, gp, t_len, 2 * hh))
    dsk = jnp.pad(dsk.reshape(ne, gp, 1, pw), ((0, 0), (0, 0), (0, 7), (0, 0)))
    return wbig, wout, dec, dsk


def _s5_kernel(uc_ref, ul_ref, wbig_ref, wout_ref, dec_ref, dsk_ref, yc_ref, yl_ref, z_ref, sin_ref, car_ref,
               *, bsz):
    tr = z_ref.shape[1]
    nb_c, nb_l = uc_ref.shape[1] // tr, ul_ref.shape[1] // tr
    nb = nb_c + nb_l
    wy = S5_T * S5_PAIR
    sw = 2 * S5_STATE
    parts = ((uc_ref, yc_ref, 0, nb_c), (ul_ref, yl_ref, nb_c, nb_l))

    for u_ref, _, base, nblk in parts:
        def stage1(i, carry, u_ref=u_ref, base=base):
            r = pl.multiple_of(i * tr, tr)
            ub = u_ref[0, pl.ds(r, tr), :].astype(BF16)
            z_ref[base + i] = jnp.dot(ub, wbig_ref[0, :, wy:], preferred_element_type=F32)
            return carry

        lax.fori_loop(0, nblk, stage1, 0)

    dec = dec_ref[0]
    entry_gain = []
    for d in range(2):
        a_re = dec[2 * d:2 * d + 1][None]
        a_im = dec[2 * d + 1:2 * d + 2][None]
        lc = 2 * d * sw
        sc = 2 * d * sw
        powers = [(jnp.ones_like(a_re), jnp.zeros_like(a_re))]
        for _ in range(S5_BLOCK):
            p_re, p_im = powers[-1]
            powers.append((p_re * a_re - p_im * a_im, p_re * a_im + p_im * a_re))
        steps = list(range(S5_BLOCK)) if d == 0 else list(range(S5_BLOCK - 1, -1, -1))
        if d == 0:
            block_order = list(range(nb))
        else:
            block_order = list(range(nb_c - 1, -1, -1)) + list(range(nb - 1, nb_c - 1, -1))

        s_re = jnp.zeros((nb, bsz, sw), F32)
        s_im = jnp.zeros((nb, bsz, sw), F32)
        for i in steps:
            rs = slice(i * bsz, (i + 1) * bsz)
            sin_ref[:, rs, sc:sc + sw] = s_re
            sin_ref[:, rs, sc + sw:sc + 2 * sw] = s_im
            l_re = z_ref[:, rs, lc:lc + sw]
            l_im = z_ref[:, rs, lc + sw:lc + 2 * sw]
            s_re, s_im = a_re * s_re - a_im * s_im + l_re, a_re * s_im + a_im * s_re + l_im

        g_re, g_im = powers[S5_BLOCK]
        c_re = jnp.zeros((bsz, sw), F32)
        c_im = jnp.zeros((bsz, sw), F32)
        for blk in block_order:
            car_ref[blk, :, sc:sc + sw] = c_re
            car_ref[blk, :, sc + sw:sc + 2 * sw] = c_im
            c_re, c_im = (g_re[0] * c_re - g_im[0] * c_im + s_re[blk], g_re[0] * c_im + g_im[0] * c_re + s_im[blk])

        since = {i: n for n, i in enumerate(steps)}
        entry_gain.append(tuple(
            jnp.concatenate([jnp.broadcast_to(powers[since[i]][part][0], (bsz, sw)) for i in range(S5_BLOCK)], axis=0)
            for part in range(2)))

    dsk = dsk_ref[0][0:1]

    for u_ref, y_ref, base, nblk in parts:
        def stage3(i, carry, u_ref=u_ref, y_ref=y_ref, base=base):
            r = pl.multiple_of(i * tr, tr)
            u = u_ref[0, pl.ds(r, tr), :]
            states = []
            for d in range(2):
                sc = 2 * d * sw
                p_re, p_im = entry_gain[d]
                e_re = jnp.concatenate([car_ref[base + i, :, sc:sc + sw]] * S5_BLOCK, axis=0)
                e_im = jnp.concatenate([car_ref[base + i, :, sc + sw:sc + 2 * sw]] * S5_BLOCK, axis=0)
                states.append(sin_ref[base + i, :, sc:sc + sw] + (p_re * e_re - p_im * e_im))
                states.append(sin_ref[base + i, :, sc + sw:sc + 2 * sw] + (p_re * e_im + p_im * e_re))
            sb = jnp.concatenate(states, axis=1).astype(BF16)
            y = (dsk * u + jnp.dot(u.astype(BF16), wbig_ref[0, :, 0:wy], preferred_element_type=F32)
                 + jnp.dot(sb, wout_ref[0], preferred_element_type=F32))
            y_ref[0, pl.ds(r, tr), :] = y
            return carry

        lax.fori_loop(0, nblk, stage3, 0)


def _s5_chunked(u_ctx, u_lat, wbig, wout, dec, dsk, e, bsz):
    gp, rows_c, width = u_ctx.shape
    rows_l = u_lat.shape[1]
    zc = wbig.shape[-1]
    tr = S5_BLOCK * bsz
    nb = (rows_c + rows_l) // tr
    per_pair = lambda shape: pl.BlockSpec((1,) + shape[1:], lambda q: (q, 0, 0))
    of_layer = lambda w: pl.BlockSpec((None, 1) + w.shape[2:], lambda q: (e, q, 0, 0))
    return pl.pallas_call(
        functools.partial(_s5_kernel, bsz=bsz),
        out_shape=(jax.ShapeDtypeStruct(u_ctx.shape, F32), jax.ShapeDtypeStruct(u_lat.shape, F32)),
        grid=(gp,),
        in_specs=[per_pair(u_ctx.shape), per_pair(u_lat.shape), of_layer(wbig), of_layer(wout),
                  of_layer(dec), of_layer(dsk)],
        out_specs=(per_pair(u_ctx.shape), per_pair(u_lat.shape)),
        scratch_shapes=[pltpu.VMEM((nb, tr, zc - width), F32), pltpu.VMEM((nb, tr, width), F32),
                        pltpu.VMEM((nb, bsz, 8 * S5_STATE), F32)],
        compiler_params=_params(),
        name="s5_chunked",
    )(u_ctx, u_lat, wbig, wout, dec, dsk)


def _attend_pairs(q_ref, k_refs, v_refs, bias_ref, o_ref, n_biased):
    width = q_ref.shape[-1]
    kb = k_refs[0].shape[1]
    lane = lax.broadcasted_iota(jnp.int32, (1, 2 * NA_HEAD_DIM), 1)
    outs = []
    for hp in range(width // (2 * NA_HEAD_DIM)):
        cs = slice(hp * 2 * NA_HEAD_DIM, (hp + 1) * 2 * NA_HEAD_DIM)
        qp = q_ref[0, :, cs]
        ks = [r[0, :, cs] for r in k_refs]
        vs = [r[0, :, cs] for r in v_refs]
        res = []
        for e in range(2):
            sel = (lane // NA_HEAD_DIM) == e
            qe = jnp.where(sel, qp, jnp.zeros_like(qp))
            parts = []
            for i, kk in enumerate(ks):
                s = lax.dot_general(qe, kk, (((1,), (1,)), ((), ())), preferred_element_type=F32)
                if i < n_biased:
                    s = s + bias_ref[0, 2 * hp + e, :, i * kb:(i + 1) * kb]
                parts.append(s)
            m = parts[0].max(axis=-1, keepdims=True)
            for s in parts[1:]:
                m = jnp.maximum(m, s.max(axis=-1, keepdims=True))
            ps = [jnp.exp(s - m) for s in parts]
            den = ps[0].sum(axis=-1, keepdims=True)
            for p in ps[1:]:
                den = den + p.sum(axis=-1, keepdims=True)
            acc = jnp.dot(ps[0].astype(BF16), vs[0], preferred_element_type=F32)
            for p, vv in zip(ps[1:], vs[1:]):
                acc = acc + jnp.dot(p.astype(BF16), vv, preferred_element_type=F32)
            res.append(acc / den)
        outs.append(jnp.where(lane < NA_HEAD_DIM, res[0], res[1]))
    o_ref[0] = jnp.concatenate(outs, axis=-1).astype(o_ref.dtype)


def _na_lat_kernel(q_ref, k0, k1, k2, v0, v1, v2, kc, vc, bias_ref, o_ref):
    _attend_pairs(q_ref, (k0, k1, k2, kc), (v0, v1, v2, vc), bias_ref, o_ref, 3)


def _na_ctx_kernel(q_ref, kc, vc, o_ref):
    _attend_pairs(q_ref, (kc,), (vc,), None, o_ref, 0)


def _na_bias_table(rpb, rows):
    nh = rpb.shape[0]
    nblk = rows // NA_QROWS
    blocks = np.array([0, 1, nblk - 1])
    kb0 = np.clip(blocks - 1, 0, nblk - NA_KROWS // NA_QROWS) * NA_QROWS
    qr = blocks[:, None] * NA_QROWS + np.arange(NA_QROWS)[None, :]
    r0 = np.clip(qr - NA_WIN_R // 2, 0, rows - NA_WIN_R)
    kr = kb0[:, None] + np.arange(NA_KROWS)[None, :]
    row_ok = (kr[:, None, :] >= r0[:, :, None]) & (kr[:, None, :] < r0[:, :, None] + NA_WIN_R)
    n_dr = 2 * NA_WIN_R - 1
    dr = kr[:, None, :] - qr[:, :, None] + (NA_WIN_R - 1)
    cols = np.arange(GRID_W)
    c0 = np.clip(cols - NA_WIN_C // 2, 0, GRID_W - NA_WIN_C)
    col_ok = (cols[None, :] >= c0[:, None]) & (cols[None, :] < c0[:, None] + NA_WIN_C)
    dc = np.clip(cols[None, :] - cols[:, None], -(NA_WIN_C - 1), NA_WIN_C - 1) + (NA_WIN_C - 1)
    sel_c = (np.arange(2 * NA_WIN_C - 1)[:, None, None] == dc[None]).astype(np.float32)
    by_col = jnp.einsum('hrc,cqk->hrqk', rpb.astype(F32), sel_c, precision=HIGHEST)
    zero = jnp.zeros((nh, 1, GRID_W, GRID_W), F32)
    padded = jnp.concatenate([zero, by_col, zero], axis=1)
    two_rows = jnp.concatenate([padded[:, :-1], padded[:, 1:]], axis=-1)
    variants = []
    for v in range(len(blocks)):
        strips = []
        for i in range(NA_QROWS):
            first = np.clip(dr[v, i, 0::2], -1, n_dr - 1) + 1
            strips.append(jnp.concatenate([two_rows[:, int(s)] for s in first], axis=-1))
        variants.append(jnp.concatenate(strips, axis=1))
    bias = jnp.stack(variants, axis=0)
    ok = (row_ok[:, :, None, :, None] & col_ok[None, None, :, None, :]).reshape(
        len(blocks), 1, NA_QROWS * GRID_W, NA_KROWS * GRID_W)
    return jnp.where(ok, bias, NEG_INF)


def _na_latent(qkv, qkv_c, bias, e):
    bsz, seq, w3 = qkv.shape
    width = w3 // 3
    nh = width // NA_HEAD_DIM
    lc = qkv_c.shape[1]
    qb = NA_QROWS * GRID_W
    nblk = seq // qb
    nkb = NA_KROWS // NA_QROWS

    def kv_spec(col, s):
        return pl.BlockSpec((1, qb, width),
                            lambda b, a: (b, jnp.clip(a - 1, 0, nblk - nkb) + s, col))

    def variant(a):
        return (a > 0).astype(jnp.int32) + (a == nblk - 1).astype(jnp.int32)

    return pl.pallas_call(
        _na_lat_kernel,
        out_shape=jax.ShapeDtypeStruct((bsz, seq, width), BF16),
        grid=(bsz, nblk),
        in_specs=[pl.BlockSpec((1, qb, width), lambda b, a: (b, a, 0))]
        + [kv_spec(1, s) for s in range(nkb)] + [kv_spec(2, s) for s in range(nkb)]
        + [pl.BlockSpec((1, lc, width), lambda b, a: (b, 0, 1)),
           pl.BlockSpec((1, lc, width), lambda b, a: (b, 0, 2)),
           pl.BlockSpec((1, nh) + bias.shape[2:], lambda b, a: (variant(a), e, 0, 0))],
        out_specs=pl.BlockSpec((1, qb, width), lambda b, a: (b, a, 0)),
        compiler_params=_params(),
        name="na_latent",
    )(qkv, *([qkv] * (2 * nkb)), qkv_c, qkv_c, bias)


def _na_context(qkv_c):
    bsz, lc, w3 = qkv_c.shape
    width = w3 // 3
    return pl.pallas_call(
        _na_ctx_kernel,
        out_shape=jax.ShapeDtypeStruct((bsz, lc, width), BF16),
        grid=(bsz,),
        in_specs=[pl.BlockSpec((1, lc, width), lambda b: (b, 0, 0)),
                  pl.BlockSpec((1, lc, width), lambda b: (b, 0, 1)),
                  pl.BlockSpec((1, lc, width), lambda b: (b, 0, 2))],
        out_specs=pl.BlockSpec((1, lc, width), lambda b: (b, 0, 0)),
        compiler_params=_params(),
        name="na_context",
    )(qkv_c, qkv_c, qkv_c)


def _residual_mlp(out_l, h_ref, mod_ref, gains, w1_ref, w2_ref, o_ref):
    bsz, tt, _ = h_ref.shape
    parts = []
    for b in range(bsz):
        m = mod_ref[b]
        h1 = h_ref[b] + m[2:3] * _rms(out_l[b * tt:(b + 1) * tt], gains[1:2])
        o_ref[b] = h1
        parts.append((_rms(h1, gains[2:3]) * (1.0 + m[4:5]) + m[3:4]).astype(BF16))
    hf = jnp.concatenate(parts, axis=0)
    d_ff = w1_ref.shape[1]
    acc = None
    for kf in range(d_ff // FF_CHUNK):
        cs = slice(kf * FF_CHUNK, (kf + 1) * FF_CHUNK)
        hid = jnp.maximum(jnp.dot(hf, w1_ref[:, cs], preferred_element_type=F32), 0.0)
        part = jnp.dot((hid * hid).astype(BF16), w2_ref[cs, :], preferred_element_type=F32)
        acc = part if acc is None else acc + part
    for b in range(bsz):
        o_ref[b] = o_ref[b] + mod_ref[b][5:6] * _rms(acc[b * tt:(b + 1) * tt], gains[3:4])


def _post_even_kernel(yp_ref, na_ref, h_ref, mod_ref, g_ref, wglu_ref, wo_ref, w1_ref, w2_ref, o_ref, ys_ref):
    bsz, tt, na_w = na_ref.shape
    _pairs_to_tokens(yp_ref, ys_ref, bsz)
    g = _gelu_tanh(jnp.concatenate([ys_ref[blk] for blk in range(ys_ref.shape[0])], axis=-1))
    gate = _sigmoid(jnp.dot(g.astype(BF16), wglu_ref[...], preferred_element_type=F32))
    s5 = (g * gate).astype(BF16)
    sw = s5.shape[1]
    out_l = (jnp.dot(s5, wo_ref[0:sw, :], preferred_element_type=F32)
             + jnp.dot(na_ref[...].reshape(bsz * tt, na_w), wo_ref[sw:, :], preferred_element_type=F32))
    _residual_mlp(out_l, h_ref, mod_ref, g_ref[...], w1_ref, w2_ref, o_ref)


def _post_odd_kernel(p_ref, q_ref, h_ref, mod_ref, g_ref, wp_ref, wq_ref, wn_ref, w1_ref, w2_ref, o_ref):
    bsz, tt, _ = h_ref.shape
    half = wp_ref.shape[0]
    pc = jnp.concatenate([p_ref[:, b * half:(b + 1) * half] for b in range(bsz)], axis=0)
    qc = jnp.concatenate([q_ref[:, b * half:(b + 1) * half] for b in range(bsz)], axis=0)
    out_l = (jnp.dot(pc, wp_ref[...], preferred_element_type=F32)
             + jnp.dot(qc, wq_ref[...], preferred_element_type=F32))
    lane = lax.broadcasted_iota(jnp.int32, (tt, LANES), 1)
    nyq_blk = p_ref[:, bsz * half:].astype(F32)
    nyq = jnp.concatenate([jnp.sum(jnp.where(lane == b, nyq_blk, 0.0), axis=1, keepdims=True)
                           for b in range(bsz)], axis=0)
    out_l = out_l + nyq * wn_ref[0:1, :]
    _residual_mlp(out_l, h_ref, mod_ref, g_ref[...], w1_ref, w2_ref, o_ref)


def _post_even(y_pairs, na, h, mod, gains, w_glu, w_out, w1, w2, layer, e):
    bsz, seq, d = h.shape
    npair = y_pairs.shape[0]
    pw = S5_T * S5_PAIR
    prow = TOK_TILE // S5_T * bsz
    return pl.pallas_call(
        _post_even_kernel,
        out_shape=jax.ShapeDtypeStruct(h.shape, F32),
        grid=_token_grid(seq, TOK_TILE),
        in_specs=[pl.BlockSpec((npair, prow, pw), lambda i: (0, i, 0)), _tok_spec(bsz, na.shape[-1], TOK_TILE),
                  _tok_spec(bsz, d, TOK_TILE), _layer_spec(mod, layer), _layer_spec(gains, layer),
                  _layer_spec(w_glu, e), _layer_spec(w_out, e), _layer_spec(w1, layer), _layer_spec(w2, layer)],
        out_specs=_tok_spec(bsz, d, TOK_TILE),
        scratch_shapes=[pltpu.VMEM((npair * S5_PAIR // LANES, bsz * TOK_TILE, LANES), F32)],
        compiler_params=_params(),
        name="post_even",
    )(y_pairs, na, h, mod, gains, w_glu, w_out, w1, w2)


def _post_odd(p, q, h, mod, gains, w_p, w_q, w_n, w1, w2, layer, o):
    bsz, seq, d = h.shape
    tile = TOK_TILE_POST_ODD
    rows = lambda arr: pl.BlockSpec((tile, arr.shape[1]), lambda i: (i, 0))
    return pl.pallas_call(
        _post_odd_kernel,
        out_shape=jax.ShapeDtypeStruct(h.shape, F32),
        grid=_token_grid(seq, tile),
        in_specs=[rows(p), rows(q), _tok_spec(bsz, d, tile),
                  _layer_spec(mod, layer), _layer_spec(gains, layer),
                  _layer_spec(w_p, o), _layer_spec(w_q, o), _layer_spec(w_n, o),
                  _layer_spec(w1, layer), _layer_spec(w2, layer)],
        out_specs=_tok_spec(bsz, d, tile),
        compiler_params=_params(),
        name="post_odd",
    )(p, q, h, mod, gains, w_p, w_q, w_n, w1, w2)


def _fnet_seq_dft(dft, a, b, tm, tk):
    seq = dft.shape[0]
    p = _matmul(dft, a, tm, a.shape[1] // 3, tk, BF16)
    q = _matmul(dft, b, tm, b.shape[1] // 2, tk, BF16, a_col0=seq // tk)
    return p, q


def kernel(x, c, ctx, c_ctx, w_mod, b_mod, norm_g, w_in, w_out_even, s5_lam_re, s5_lam_im, s5_log_dt,
           s5_b_re, s5_b_im, s5_c_re, s5_c_im, s5_d, s5_w_glu, na_rpb, w_fourier, w_ff1, w_ff2):
    bsz, seq, d = x.shape
    lc = ctx.shape[1]
    depth = w_mod.shape[0]
    s5w = s5_d.shape[-1]
    last_ctx_layer = 2 * ((depth - 1) // 2)

    mod_rows = 16
    a = jnp.concatenate([c, c_ctx[None, :], jnp.zeros((mod_rows - bsz - 1, d), F32)], axis=0)
    mod = _modulation(a, w_mod, b_mod)
    pad_mod = lambda m: jnp.pad(m, ((0, 0), (0, 0), (0, 8 - N_MOD), (0, 0)))
    mod_l = pad_mod(mod[:, :bsz].reshape(depth, bsz, N_MOD, d))
    mod_c = pad_mod(jnp.broadcast_to(mod[:, bsz:bsz + 1].reshape(depth, 1, N_MOD, d), (depth, bsz, N_MOD, d)))
    gains = jnp.pad(norm_g.astype(F32), ((0, 0), (0, 4), (0, 0)))

    na_w = (w_in.shape[-1] - s5w) // 3
    qscale = jnp.concatenate([jnp.ones((s5w,), F32), jnp.full((na_w,), NA_HEAD_DIM ** -0.5, F32),
                              jnp.ones((2 * na_w,), F32)])
    w_in_b = (w_in * qscale).astype(BF16)
    w_out_b = w_out_even.astype(BF16)
    w_glu_b = s5_w_glu.astype(BF16)
    half = d // 2
    wf_lo = w_fourier[:, 1:half]
    wf_hi = w_fourier[:, half + 1:][:, ::-1]
    w_p_b = jnp.concatenate([w_fourier[:, 0:1], wf_lo + wf_hi], axis=1).astype(BF16)
    w_q_b = jnp.concatenate([jnp.zeros_like(w_fourier[:, 0:1]), wf_lo - wf_hi], axis=1).astype(BF16)
    w_n = jnp.pad(w_fourier[:, half:half + 1].astype(F32), ((0, 0), (0, 7), (0, 0)))
    w1_b = w_ff1.astype(BF16)
    w2_b = w_ff2.astype(BF16)

    wbig, wout, dec, dsk = _s5_prepare(s5_lam_re, s5_lam_im, s5_log_dt, s5_b_re, s5_b_im,
                                       s5_c_re, s5_c_im, s5_d)
    dft_d = _dft_pair(d, right=True)
    dft_dh = jnp.concatenate([dft_d[:, :half], dft_d[:, d:d + half],
                              jnp.broadcast_to(dft_d[:, half:half + 1], (d, LANES))], axis=1)
    dft_l = _dft_pair(seq, right=False)
    dft_c = _dft_pair(lc, right=False)

    bias = _na_bias_table(na_rpb.reshape((-1,) + na_rpb.shape[2:]), seq // GRID_W)

    h, s = x, ctx
    for layer in range(depth):
        upd_ctx = layer < last_ctx_layer
        if layer % 2 == 0:
            e = layer // 2
            u_l, qkv_l = _pre_even(h, mod_l, gains, w_in_b, layer, e)
            u_c, qkv_c = _pre_even(s, mod_c, gains, w_in_b, layer, e)
            y_c, y_l = _s5_chunked(u_c, u_l, wbig, wout, dec, dsk, e, bsz)
            na_l = _na_latent(qkv_l, qkv_c, bias, e)
            h_new = _post_even(y_l, na_l, h, mod_l, gains, w_glu_b, w_out_b, w1_b, w2_b, layer, e)
            if upd_ctx:
                na_c = _na_context(qkv_c)
                s = _post_even(y_c, na_c, s, mod_c, gains, w_glu_b, w_out_b, w1_b, w2_b, layer, e)
            h = h_new
        else:
            o = layer // 2
            a_l, b_l = _pre_odd(h, mod_l, gains, dft_dh, layer)
            p_l, q_l = _fnet_seq_dft(dft_l, a_l, b_l, 1024, 1024)
            h_new = _post_odd(p_l, q_l, h, mod_l, gains, w_p_b, w_q_b, w_n, w1_b, w2_b, layer, o)
            if upd_ctx:
                a_c, b_c = _pre_odd(s, mod_c, gains, dft_dh, layer)
                p_c, q_c = _fnet_seq_dft(dft_c, a_c, b_c, lc, lc)
                s = _post_odd(p_c, q_c, s, mod_c, gains, w_p_b, w_q_b, w_n, w1_b, w2_b, layer, o)
            h = h_new
    return h
```

```python
import functools
import math

import jax
import jax.numpy as jnp
import numpy as np
from jax import lax
from jax.experimental import pallas as pl
from jax.experimental.pallas import tpu as pltpu

F32 = jnp.float32
BF16 = jnp.bfloat16
EPS = 1e-6
NEG_INF = -1e30

GRID_W = 64
S5_GROUP = 16
S5_STATE = 64
S5_T = 16
S5_PAIR = 2 * S5_GROUP
S5_BLOCK = 16
LANES = 128
TOK_TILE = 128
FF_CHUNK = 1024
NA_HEAD_DIM = 64
NA_WIN_R = 8
NA_WIN_C = 16
NA_QROWS = 4
NA_KROWS = 12
N_MOD = 6

V7X_VMEM_LIMIT_BYTES = 56 * 1024 * 1024
HIGHEST = lax.Precision.HIGHEST


def _params(**kw):
    return pltpu.CompilerParams(vmem_limit_bytes=V7X_VMEM_LIMIT_BYTES, **kw)


def _resident(shape):
    nd = len(shape)
    return pl.BlockSpec(shape, lambda *_: (0,) * nd, pipeline_mode=pl.Buffered(1))


def _layer_spec(stacked, idx):
    nd = stacked.ndim
    return pl.BlockSpec((None,) + stacked.shape[1:], lambda *_: (idx,) + (0,) * (nd - 1),
                        pipeline_mode=pl.Buffered(1))


def _rms(x, g):
    ms = jnp.mean(x * x, axis=-1, keepdims=True)
    return x * lax.rsqrt(ms + EPS) * g


def _sigmoid(x):
    return 1.0 / (1.0 + jnp.exp(-x))


def _gelu_tanh(x):
    c = math.sqrt(2.0 / math.pi)
    return x * (0.5 * (1.0 + jnp.tanh(c * (x + 0.044715 * (x * x * x)))))


def _mod_kernel(a_ref, w_ref, b_ref, o_ref):
    a = a_ref[...]
    act = a * _sigmoid(a)
    o_ref[0] = jnp.dot(act.astype(BF16), w_ref[0].astype(BF16), preferred_element_type=F32) + b_ref[0]


def _modulation(a, w_mod, b_mod):
    depth, d, n = w_mod.shape
    rows = a.shape[0]
    tn = 1536
    return pl.pallas_call(
        _mod_kernel,
        out_shape=jax.ShapeDtypeStruct((depth, rows, n), F32),
        grid=(depth, n // tn),
        in_specs=[
            pl.BlockSpec((rows, d), lambda l, j: (0, 0)),
            pl.BlockSpec((1, d, tn), lambda l, j: (l, 0, j)),
            pl.BlockSpec((1, 1, tn), lambda l, j: (l, 0, j)),
        ],
        out_specs=pl.BlockSpec((1, rows, tn), lambda l, j: (l, 0, j)),
        compiler_params=_params(),
        name="adaln_mod",
    )(a, w_mod, b_mod.reshape(depth, 1, n))


def _lane_quarter(shape):
    return lax.broadcasted_iota(jnp.int32, shape, 1) // S5_PAIR


def _tokens_to_pairs(tok_ref, pair_ref, bsz):
    per = LANES // S5_PAIR
    tt = tok_ref.shape[1] // bsz
    n_chunks = tt // S5_T
    quarter = _lane_quarter((n_chunks * bsz, LANES))
    for blk in range(tok_ref.shape[0]):
        for m in range(S5_T // per):
            src = [jnp.concatenate([tok_ref[blk, pl.ds(c * S5_T + per * m + jj, bsz, stride=tt), :]
                                    for c in range(n_chunks)], axis=0) for jj in range(per)]
            for qq in range(per):
                dest = None
                for jj in range(per):
                    shift = ((jj - qq) % per) * S5_PAIR
                    r = src[jj] if shift == 0 else pltpu.roll(src[jj], shift, axis=1)
                    dest = r if dest is None else jnp.where(quarter == jj, r, dest)
                pair_ref[per * blk + qq, :, m * LANES:(m + 1) * LANES] = dest


def _pairs_to_tokens(pair_ref, tok_ref, bsz):
    per = LANES // S5_PAIR
    tt = tok_ref.shape[1] // bsz
    n_chunks = tt // S5_T
    quarter = _lane_quarter((n_chunks * bsz, LANES))
    for blk in range(tok_ref.shape[0]):
        for m in range(S5_T // per):
            src = [pair_ref[per * blk + qq, :, m * LANES:(m + 1) * LANES] for qq in range(per)]
            for ts in range(per):
                dest = None
                for qq in range(per):
                    shift = ((qq - ts) % per) * S5_PAIR
                    r = src[qq] if shift == 0 else pltpu.roll(src[qq], shift, axis=1)
                    dest = r if dest is None else jnp.where(quarter == qq, r, dest)
                for c in range(n_chunks):
                    tok_ref[blk, pl.ds(c * S5_T + per * m + ts, bsz, stride=tt), :] = dest[c * bsz:(c + 1) * bsz]


def _modulated(h_ref, mod_ref, gain, shift_row, scale_row):
    parts = []
    for b in range(h_ref.shape[0]):
        m = mod_ref[b]
        y = _rms(h_ref[b], gain) * (1.0 + m[scale_row:scale_row + 1]) + m[shift_row:shift_row + 1]
        parts.append(y.astype(BF16))
    return jnp.concatenate(parts, axis=0)


def _pre_even_kernel(h_ref, mod_ref, g_ref, w_ref, up_ref, qkv_ref, us_ref, *, s5_width):
    bsz, tt, _ = h_ref.shape
    hl = _modulated(h_ref, mod_ref, g_ref[0:1], 0, 1)
    z = jnp.dot(hl, w_ref[...], preferred_element_type=F32)
    for blk in range(s5_width // LANES):
        us_ref[blk] = z[:, blk * LANES:(blk + 1) * LANES]
    for b in range(bsz):
        qkv_ref[b] = z[b * tt:(b + 1) * tt, s5_width:].astype(BF16)
    _tokens_to_pairs(us_ref, up_ref, bsz)


def _token_grid(seq, tile):
    return (seq // tile,)


def _tok_spec(bsz, width, tile):
    return pl.BlockSpec((bsz, tile, width), lambda i: (0, i, 0))


def _pre_even(h, mod, gains, w_in, layer, e):
    bsz, seq, d = h.shape
    n = w_in.shape[-1]
    s5w = d // 2
    npair = s5w // S5_PAIR
    pw = S5_T * S5_PAIR
    tile = TOK_TILE
    prow = tile // S5_T * bsz
    return pl.pallas_call(
        functools.partial(_pre_even_kernel, s5_width=s5w),
        out_shape=(jax.ShapeDtypeStruct((npair, seq // S5_T * bsz, pw), F32),
                   jax.ShapeDtypeStruct((bsz, seq, n - s5w), BF16)),
        grid=_token_grid(seq, tile),
        in_specs=[_tok_spec(bsz, d, tile), _layer_spec(mod, layer), _layer_spec(gains, layer),
                  _layer_spec(w_in, e)],
        out_specs=(pl.BlockSpec((npair, prow, pw), lambda i: (0, i, 0)), _tok_spec(bsz, n - s5w, tile)),
        scratch_shapes=[pltpu.VMEM((s5w // LANES, bsz * tile, LANES), F32)],
        compiler_params=_params(),
        name="pre_even",
    )(h, mod, gains, w_in)


def _pre_odd_kernel(h_ref, mod_ref, g_ref, w_ref, a_ref, b_ref, *, half):
    bsz, tt, _ = h_ref.shape
    hl = _modulated(h_ref, mod_ref, g_ref[0:1], 0, 1)
    z = jnp.dot(hl, w_ref[...], preferred_element_type=F32)
    lane = lax.broadcasted_iota(jnp.int32, (tt, LANES), 1)
    nyq = jnp.zeros((tt, LANES), F32)
    for b in range(bsz):
        zb = z[b * tt:(b + 1) * tt]
        a_ref[:, b * half:(b + 1) * half] = zb[:, :half].astype(BF16)
        b_ref[:, b * half:(b + 1) * half] = zb[:, half:2 * half].astype(BF16)
        nyq = jnp.where(lane == b, zb[:, 2 * half:], nyq)
    a_ref[:, bsz * half:] = nyq.astype(BF16)


def _pre_odd(h, mod, gains, w_cs, layer):
    bsz, seq, d = h.shape
    half = d // 2
    return pl.pallas_call(
        functools.partial(_pre_odd_kernel, half=half),
        out_shape=(jax.ShapeDtypeStruct((seq, bsz * half + LANES), BF16),
                   jax.ShapeDtypeStruct((seq, bsz * half), BF16)),
        grid=_token_grid(seq, TOK_TILE),
        in_specs=[_tok_spec(bsz, d, TOK_TILE), _layer_spec(mod, layer), _layer_spec(gains, layer),
                  _resident(w_cs.shape)],
        out_specs=(pl.BlockSpec((TOK_TILE, bsz * half + LANES), lambda i: (i, 0)),
                   pl.BlockSpec((TOK_TILE, bsz * half), lambda i: (i, 0))),
        compiler_params=_params(),
        name="pre_odd",
    )(h, mod, gains, w_cs)


def _matmul_kernel(a_ref, b_ref, o_ref, acc_ref, *, nk):
    k = pl.program_id(2)

    @pl.when(k == 0)
    def _():
        acc_ref[...] = jnp.zeros_like(acc_ref)

    acc_ref[...] += jnp.dot(a_ref[...], b_ref[...], preferred_element_type=F32)

    @pl.when(k == nk - 1)
    def _():
        o_ref[...] = acc_ref[...].astype(o_ref.dtype)


def _matmul(a, b, tm, tn, tk, out_dtype, a_col0=0):
    m = a.shape[0]
    kdim, n = b.shape
    nk = kdim // tk
    return pl.pallas_call(
        functools.partial(_matmul_kernel, nk=nk),
        out_shape=jax.ShapeDtypeStruct((m, n), out_dtype),
        grid=(m // tm, n // tn, nk),
        in_specs=[pl.BlockSpec((tm, tk), lambda i, j, k: (i, k + a_col0)),
                  pl.BlockSpec((tk, tn), lambda i, j, k: (k, j))],
        out_specs=pl.BlockSpec((tm, tn), lambda i, j, k: (i, j)),
        scratch_shapes=[pltpu.VMEM((tm, tn), F32)],
        compiler_params=_params(),
        name="fnet_seq_dft",
    )(a, b)


def _dft_pair(n, right):
    n0 = LANES
    n1 = n // n0
    k = jnp.arange(n, dtype=jnp.int32)[:, None]
    pa = (k * (jnp.arange(n1, dtype=jnp.int32)[None, :] * n0)) % n
    pb = (k * jnp.arange(n0, dtype=jnp.int32)[None, :]) % n
    w = 2.0 * math.pi / n
    scale = 1.0 / math.sqrt(n)
    sign = 1.0 if right else -1.0
    ca, sa = jnp.cos(pa.astype(F32) * w) * scale, jnp.sin(pa.astype(F32) * w) * scale
    cb, sb = jnp.cos(pb.astype(F32) * w), jnp.sin(pb.astype(F32) * w)
    a1 = jnp.concatenate([ca, sign * sa], axis=1)
    a2 = jnp.concatenate([sa, -sign * ca], axis=1)
    tr = min(n, 256)
    rows = lambda w_: pl.BlockSpec((tr, w_), lambda i: (i, 0))
    return pl.pallas_call(
        _dft_expand_kernel,
        out_shape=jax.ShapeDtypeStruct((n, 2 * n), BF16),
        grid=(n // tr,),
        in_specs=[rows(2 * n1), rows(2 * n1), rows(n0), rows(n0)],
        out_specs=rows(2 * n),
        compiler_params=_params(),
        name="dft_expand",
    )(a1, a2, cb, sb)


def _dft_expand_kernel(a1_ref, a2_ref, cb_ref, sb_ref, o_ref):
    cb, sb = cb_ref[...], sb_ref[...]
    for j in range(a1_ref.shape[1]):
        blk = a1_ref[:, j:j + 1] * cb - a2_ref[:, j:j + 1] * sb
        o_ref[:, j * LANES:(j + 1) * LANES] = blk.astype(BF16)


def _s5_prepare(lam_re, lam_im, log_dt, b_re, b_im, c_re, c_im, d_skip):
    t_len, hh, pp = S5_T, S5_GROUP, S5_STATE
    ne, _, gg, _ = lam_re.shape
    lam_re = jnp.minimum(lam_re.astype(F32), -1e-4)
    lam_im = lam_im.astype(F32)
    dt = jnp.exp(log_dt.astype(F32))[..., None]
    mag = jnp.exp(lam_re * dt)
    a_re = mag * jnp.cos(lam_im * dt)
    a_im = mag * jnp.sin(lam_im * dt)
    den = lam_re * lam_re + lam_im * lam_im
    num_re = a_re - 1.0
    f_re = (num_re * lam_re + a_im * lam_im) / den
    f_im = (a_im * lam_re - num_re * lam_im) / den
    b_re = b_re.astype(F32)
    b_im = b_im.astype(F32)
    bb_re = f_re[..., None] * b_re - f_im[..., None] * b_im
    bb_im = f_re[..., None] * b_im + f_im[..., None] * b_re
    tau = jnp.arange(t_len + 1, dtype=F32).reshape(1, 1, 1, -1, 1)
    lr, li, dtt = lam_re[..., None, :], lam_im[..., None, :], dt[..., None]
    pmag = jnp.exp(lr * dtt * tau)
    p_re = pmag * jnp.cos(li * dtt * tau)
    p_im = pmag * jnp.sin(li * dtt * tau)
    bbt_re = jnp.swapaxes(bb_re, -1, -2)[:, :, :, None]
    bbt_im = jnp.swapaxes(bb_im, -1, -2)[:, :, :, None]
    abt_re = p_re[..., None, :] * bbt_re - p_im[..., None, :] * bbt_im
    abt_im = p_re[..., None, :] * bbt_im + p_im[..., None, :] * bbt_re
    c_re = c_re.astype(F32)[:, :, :, None]
    c_im = c_im.astype(F32)[:, :, :, None]
    e_re = c_re * p_re[..., None, :] - c_im * p_im[..., None, :]
    e_im = c_re * p_im[..., None, :] + c_im * p_re[..., None, :]

    gp = gg // 2
    pw = t_len * 2 * hh

    def cat(re, im):
        tn = re.shape[2]
        re = re.reshape(ne, gp, 2, tn, hh, pp)
        im = im.reshape(ne, gp, 2, tn, hh, pp)
        z = jnp.zeros((ne, gp, tn, hh, pp), F32)
        e0 = jnp.concatenate([re[:, :, 0], z, im[:, :, 0], z], axis=-1)
        e1 = jnp.concatenate([z, re[:, :, 1], z, im[:, :, 1]], axis=-1)
        return jnp.stack([e0, e1], axis=3).reshape(ne, gp, tn * 2 * hh, 4 * pp)

    fwd_t = slice(0, t_len)
    rev_t = slice(t_len - 1, None, -1)
    st_cols, out_rows, resp = [], [], []
    for d in range(2):
        tau_k = fwd_t if d == 0 else rev_t
        bcat0 = cat(abt_re[:, d, :, 0:1], abt_im[:, d, :, 0:1])
        ecat = cat(e_re[:, d, :, tau_k], -e_im[:, d, :, tau_k])
        resp.append(jnp.einsum('xqrk,xqck->xqrc', bcat0, ecat, precision=HIGHEST))
        tau_s = rev_t if d == 0 else fwd_t
        st_cols.append(cat(abt_re[:, d, :, tau_s], abt_im[:, d, :, tau_s]))
        tau_o = slice(1, t_len + 1) if d == 0 else slice(t_len, 0, -1)
        out_rows.append(jnp.swapaxes(cat(e_re[:, d, :, tau_o], -e_im[:, d, :, tau_o]), -1, -2))

    span = pw - 2 * hh
    kf = jnp.pad(resp[0], ((0, 0), (0, 0), (0, 0), (span, 0)))
    kb = jnp.pad(resp[1], ((0, 0), (0, 0), (0, 0), (0, span)))
    blocks = []
    for j in range(t_len):
        lo_f = span - 2 * hh * j
        lo_b = 2 * hh * (t_len - 1 - j)
        blocks.append(kf[..., lo_f:lo_f + pw] + kb[..., lo_b:lo_b + pw])
    toep = jnp.stack(blocks, axis=2).reshape(ne, gp, pw, pw)
    wbig = jnp.concatenate([toep] + st_cols, axis=-1).astype(BF16)
    wout = jnp.concatenate(out_rows, axis=2).astype(BF16)

    dec_rows = [p_re[:, 0, :, t_len], p_im[:, 0, :, t_len], p_re[:, 1, :, t_len], p_im[:, 1, :, t_len]]
    dec = jnp.stack([x.reshape(ne, gp, 2 * pp) for x in dec_rows], axis=2)
    dec = jnp.pad(dec, ((0, 0), (0, 0), (0, 4), (0, 0)))
    dsk = jnp.broadcast_to(d_skip.astype(F32).reshape(ne, gp, 1, 2 * hh), (ne, gp, t_len, 2 * hh))
    dsk = jnp.pad(dsk.reshape(ne, gp, 1, pw), ((0, 0), (0, 0), (0, 7), (0, 0)))
    return wbig, wout, dec, dsk


def _s5_kernel(uc_ref, ul_ref, wbig_ref, wout_ref, dec_ref, dsk_ref, yc_ref, yl_ref, z_ref, sin_ref, car_ref,
               *, bsz):
    tr = z_ref.shape[1]
    nb_c, nb_l = uc_ref.shape[1] // tr, ul_ref.shape[1] // tr
    nb = nb_c + nb_l
    wy = S5_T * S5_PAIR
    sw = 2 * S5_STATE
    parts = ((uc_ref, yc_ref, 0, nb_c), (ul_ref, yl_ref, nb_c, nb_l))

    for u_ref, _, base, nblk in parts:
        def stage1(i, carry, u_ref=u_ref, base=base):
            r = pl.multiple_of(i * tr, tr)
            ub = u_ref[0, pl.ds(r, tr), :].astype(BF16)
            z_ref[base + i] = jnp.dot(ub, wbig_ref[0, :, wy:], preferred_element_type=F32)
            return carry

        lax.fori_loop(0, nblk, stage1, 0)

    dec = dec_ref[0]
    entry_gain = []
    for d in range(2):
        a_re = dec[2 * d:2 * d + 1][None]
        a_im = dec[2 * d + 1:2 * d + 2][None]
        lc = 2 * d * sw
        sc = 2 * d * sw
        powers = [(jnp.ones_like(a_re), jnp.zeros_like(a_re))]
        for _ in range(S5_BLOCK):
            p_re, p_im = powers[-1]
            powers.append((p_re * a_re - p_im * a_im, p_re * a_im + p_im * a_re))
        steps = list(range(S5_BLOCK)) if d == 0 else list(range(S5_BLOCK - 1, -1, -1))
        if d == 0:
            block_order = list(range(nb))
        else:
            block_order = list(range(nb_c - 1, -1, -1)) + list(range(nb - 1, nb_c - 1, -1))

        s_re = jnp.zeros((nb, bsz, sw), F32)
        s_im = jnp.zeros((nb, bsz, sw), F32)
        for i in steps:
            rs = slice(i * bsz, (i + 1) * bsz)
            sin_ref[:, rs, sc:sc + sw] = s_re
            sin_ref[:, rs, sc + sw:sc + 2 * sw] = s_im
            l_re = z_ref[:, rs, lc:lc + sw]
            l_im = z_ref[:, rs, lc + sw:lc + 2 * sw]
            s_re, s_im = a_re * s_re - a_im * s_im + l_re, a_re * s_im + a_im * s_re + l_im

        g_re, g_im = powers[S5_BLOCK]
        c_re = jnp.zeros((bsz, sw), F32)
        c_im = jnp.zeros((bsz, sw), F32)
        for blk in block_order:
            car_ref[blk, :, sc:sc + sw] = c_re
            car_ref[blk, :, sc + sw:sc + 2 * sw] = c_im
            c_re, c_im = (g_re[0] * c_re - g_im[0] * c_im + s_re[blk], g_re[0] * c_im + g_im[0] * c_re + s_im[blk])

        since = {i: n for n, i in enumerate(steps)}
        entry_gain.append(tuple(
            jnp.concatenate([jnp.broadcast_to(powers[since[i]][part][0], (bsz, sw)) for i in range(S5_BLOCK)], axis=0)
            for part in range(2)))

    dsk = dsk_ref[0][0:1]

    for u_ref, y_ref, base, nblk in parts:
        def stage3(i, carry, u_ref=u_ref, y_ref=y_ref, base=base):
            r = pl.multiple_of(i * tr, tr)
            u = u_ref[0, pl.ds(r, tr), :]
            states = []
            for d in range(2):
                sc = 2 * d * sw
                p_re, p_im = entry_gain[d]
                e_re = jnp.concatenate([car_ref[base + i, :, sc:sc + sw]] * S5_BLOCK, axis=0)
                e_im = jnp.concatenate([car_ref[base + i, :, sc + sw:sc + 2 * sw]] * S5_BLOCK, axis=0)
                states.append(sin_ref[base + i, :, sc:sc + sw] + (p_re * e_re - p_im * e_im))
                states.append(sin_ref[base + i, :, sc + sw:sc + 2 * sw] + (p_re * e_im + p_im * e_re))
            sb = jnp.concatenate(states, axis=1).astype(BF16)
            y = (dsk * u + jnp.dot(u.astype(BF16), wbig_ref[0, :, 0:wy], preferred_element_type=F32)
                 + jnp.dot(sb, wout_ref[0], preferred_element_type=F32))
            y_ref[0, pl.ds(r, tr), :] = y
            return carry

        lax.fori_loop(0, nblk, stage3, 0)


def _s5_chunked(u_ctx, u_lat, wbig, wout, dec, dsk, e, bsz):
    gp, rows_c, width = u_ctx.shape
    rows_l = u_lat.shape[1]
    zc = wbig.shape[-1]
    tr = S5_BLOCK * bsz
    nb = (rows_c + rows_l) // tr
    per_pair = lambda shape: pl.BlockSpec((1,) + shape[1:], lambda q: (q, 0, 0))
    of_layer = lambda w: pl.BlockSpec((None, 1) + w.shape[2:], lambda q: (e, q, 0, 0))
    return pl.pallas_call(
        functools.partial(_s5_kernel, bsz=bsz),
        out_shape=(jax.ShapeDtypeStruct(u_ctx.shape, F32), jax.ShapeDtypeStruct(u_lat.shape, F32)),
        grid=(gp,),
        in_specs=[per_pair(u_ctx.shape), per_pair(u_lat.shape), of_layer(wbig), of_layer(wout),
                  of_layer(dec), of_layer(dsk)],
        out_specs=(per_pair(u_ctx.shape), per_pair(u_lat.shape)),
        scratch_shapes=[pltpu.VMEM((nb, tr, zc - width), F32), pltpu.VMEM((nb, tr, width), F32),
                        pltpu.VMEM((nb, bsz, 8 * S5_STATE), F32)],
        compiler_params=_params(),
        name="s5_chunked",
    )(u_ctx, u_lat, wbig, wout, dec, dsk)


def _attend_pairs(q_ref, k_refs, v_refs, bias_ref, o_ref, n_biased):
    width = q_ref.shape[-1]
    kb = k_refs[0].shape[1]
    lane = lax.broadcasted_iota(jnp.int32, (1, 2 * NA_HEAD_DIM), 1)
    outs = []
    for hp in range(width // (2 * NA_HEAD_DIM)):
        cs = slice(hp * 2 * NA_HEAD_DIM, (hp + 1) * 2 * NA_HEAD_DIM)
        qp = q_ref[0, :, cs]
        ks = [r[0, :, cs] for r in k_refs]
        vs = [r[0, :, cs] for r in v_refs]
        res = []
        for e in range(2):
            sel = (lane // NA_HEAD_DIM) == e
            qe = jnp.where(sel, qp, jnp.zeros_like(qp))
            parts = []
            for i, kk in enumerate(ks):
                s = lax.dot_general(qe, kk, (((1,), (1,)), ((), ())), preferred_element_type=F32)
                if i < n_biased:
                    s = s + bias_ref[0, 2 * hp + e, :, i * kb:(i + 1) * kb]
                parts.append(s)
            m = parts[0].max(axis=-1, keepdims=True)
            for s in parts[1:]:
                m = jnp.maximum(m, s.max(axis=-1, keepdims=True))
            ps = [jnp.exp(s - m) for s in parts]
            den = ps[0].sum(axis=-1, keepdims=True)
            for p in ps[1:]:
                den = den + p.sum(axis=-1, keepdims=True)
            acc = jnp.dot(ps[0].astype(BF16), vs[0], preferred_element_type=F32)
            for p, vv in zip(ps[1:], vs[1:]):
                acc = acc + jnp.dot(p.astype(BF16), vv, preferred_element_type=F32)
            res.append(acc / den)
        outs.append(jnp.where(lane < NA_HEAD_DIM, res[0], res[1]))
    o_ref[0] = jnp.concatenate(outs, axis=-1).astype(o_ref.dtype)


def _na_lat_kernel(q_ref, k0, k1, k2, v0, v1, v2, kc, vc, bias_ref, o_ref):
    _attend_pairs(q_ref, (k0, k1, k2, kc), (v0, v1, v2, vc), bias_ref, o_ref, 3)


def _na_ctx_kernel(q_ref, kc, vc, o_ref):
    _attend_pairs(q_ref, (kc,), (vc,), None, o_ref, 0)


def _na_bias_table(rpb, rows):
    nh = rpb.shape[0]
    nblk = rows // NA_QROWS
    blocks = np.array([0, 1, nblk - 1])
    kb0 = np.clip(blocks - 1, 0, nblk - NA_KROWS // NA_QROWS) * NA_QROWS
    qr = blocks[:, None] * NA_QROWS + np.arange(NA_QROWS)[None, :]
    r0 = np.clip(qr - NA_WIN_R // 2, 0, rows - NA_WIN_R)
    kr = kb0[:, None] + np.arange(NA_KROWS)[None, :]
    row_ok = (kr[:, None, :] >= r0[:, :, None]) & (kr[:, None, :] < r0[:, :, None] + NA_WIN_R)
    n_dr = 2 * NA_WIN_R - 1
    dr = kr[:, None, :] - qr[:, :, None] + (NA_WIN_R - 1)
    cols = np.arange(GRID_W)
    c0 = np.clip(cols - NA_WIN_C // 2, 0, GRID_W - NA_WIN_C)
    col_ok = (cols[None, :] >= c0[:, None]) & (cols[None, :] < c0[:, None] + NA_WIN_C)
    dc = np.clip(cols[None, :] - cols[:, None], -(NA_WIN_C - 1), NA_WIN_C - 1) + (NA_WIN_C - 1)
    sel_c = (np.arange(2 * NA_WIN_C - 1)[:, None, None] == dc[None]).astype(np.float32)
    by_col = jnp.einsum('hrc,cqk->hrqk', rpb.astype(F32), sel_c, precision=HIGHEST)
    zero = jnp.zeros((nh, 1, GRID_W, GRID_W), F32)
    padded = jnp.concatenate([zero, by_col, zero], axis=1)
    two_rows = jnp.concatenate([padded[:, :-1], padded[:, 1:]], axis=-1)
    variants = []
    for v in range(len(blocks)):
        strips = []
        for i in range(NA_QROWS):
            first = np.clip(dr[v, i, 0::2], -1, n_dr - 1) + 1
            strips.append(jnp.concatenate([two_rows[:, int(s)] for s in first], axis=-1))
        variants.append(jnp.concatenate(strips, axis=1))
    bias = jnp.stack(variants, axis=0)
    ok = (row_ok[:, :, None, :, None] & col_ok[None, None, :, None, :]).reshape(
        len(blocks), 1, NA_QROWS * GRID_W, NA_KROWS * GRID_W)
    return jnp.where(ok, bias, NEG_INF)


def _na_latent(qkv, qkv_c, bias, e):
    bsz, seq, w3 = qkv.shape
    width = w3 // 3
    nh = width // NA_HEAD_DIM
    lc = qkv_c.shape[1]
    qb = NA_QROWS * GRID_W
    nblk = seq // qb
    nkb = NA_KROWS // NA_QROWS

    def kv_spec(col, s):
        return pl.BlockSpec((1, qb, width),
                            lambda b, a: (b, jnp.clip(a - 1, 0, nblk - nkb) + s, col))

    def variant(a):
        return (a > 0).astype(jnp.int32) + (a == nblk - 1).astype(jnp.int32)

    return pl.pallas_call(
        _na_lat_kernel,
        out_shape=jax.ShapeDtypeStruct((bsz, seq, width), BF16),
        grid=(bsz, nblk),
        in_specs=[pl.BlockSpec((1, qb, width), lambda b, a: (b, a, 0))]
        + [kv_spec(1, s) for s in range(nkb)] + [kv_spec(2, s) for s in range(nkb)]
        + [pl.BlockSpec((1, lc, width), lambda b, a: (b, 0, 1)),
           pl.BlockSpec((1, lc, width), lambda b, a: (b, 0, 2)),
           pl.BlockSpec((1, nh) + bias.shape[2:], lambda b, a: (variant(a), e, 0, 0))],
        out_specs=pl.BlockSpec((1, qb, width), lambda b, a: (b, a, 0)),
        compiler_params=_params(),
        name="na_latent",
    )(qkv, *([qkv] * (2 * nkb)), qkv_c, qkv_c, bias)


def _na_context(qkv_c):
    bsz, lc, w3 = qkv_c.shape
    width = w3 // 3
    return pl.pallas_call(
        _na_ctx_kernel,
        out_shape=jax.ShapeDtypeStruct((bsz, lc, width), BF16),
        grid=(bsz,),
        in_specs=[pl.BlockSpec((1, lc, width), lambda b: (b, 0, 0)),
                  pl.BlockSpec((1, lc, width), lambda b: (b, 0, 1)),
                  pl.BlockSpec((1, lc, width), lambda b: (b, 0, 2))],
        out_specs=pl.BlockSpec((1, lc, width), lambda b: (b, 0, 0)),
        compiler_params=_params(),
        name="na_context",
    )(qkv_c, qkv_c, qkv_c)


def _residual_mlp(out_l, h_ref, mod_ref, gains, w1_ref, w2_ref, o_ref):
    bsz, tt, _ = h_ref.shape
    parts = []
    for b in range(bsz):
        m = mod_ref[b]
        h1 = h_ref[b] + m[2:3] * _rms(out_l[b * tt:(b + 1) * tt], gains[1:2])
        o_ref[b] = h1
        parts.append((_rms(h1, gains[2:3]) * (1.0 + m[4:5]) + m[3:4]).astype(BF16))
    hf = jnp.concatenate(parts, axis=0)
    d_ff = w1_ref.shape[1]
    acc = None
    for kf in range(d_ff // FF_CHUNK):
        cs = slice(kf * FF_CHUNK, (kf + 1) * FF_CHUNK)
        hid = jnp.maximum(jnp.dot(hf, w1_ref[:, cs], preferred_element_type=F32), 0.0)
        part = jnp.dot((hid * hid).astype(BF16), w2_ref[cs, :], preferred_element_type=F32)
        acc = part if acc is None else acc + part
    for b in range(bsz):
        o_ref[b] = o_ref[b] + mod_ref[b][5:6] * _rms(acc[b * tt:(b + 1) * tt], gains[3:4])


def _post_even_kernel(yp_ref, na_ref, h_ref, mod_ref, g_ref, wglu_ref, wo_ref, w1_ref, w2_ref, o_ref, ys_ref):
    bsz, tt, na_w = na_ref.shape
    _pairs_to_tokens(yp_ref, ys_ref, bsz)
    g = _gelu_tanh(jnp.concatenate([ys_ref[blk] for blk in range(ys_ref.shape[0])], axis=-1))
    gate = _sigmoid(jnp.dot(g.astype(BF16), wglu_ref[...], preferred_element_type=F32))
    s5 = (g * gate).astype(BF16)
    sw = s5.shape[1]
    out_l = (jnp.dot(s5, wo_ref[0:sw, :], preferred_element_type=F32)
             + jnp.dot(na_ref[...].reshape(bsz * tt, na_w), wo_ref[sw:, :], preferred_element_type=F32))
    _residual_mlp(out_l, h_ref, mod_ref, g_ref[...], w1_ref, w2_ref, o_ref)


def _post_odd_kernel(p_ref, q_ref, h_ref, mod_ref, g_ref, wp_ref, wq_ref, wn_ref, w1_ref, w2_ref, o_ref):
    bsz, tt, _ = h_ref.shape
    half = wp_ref.shape[0]
    pc = jnp.concatenate([p_ref[:, b * half:(b + 1) * half] for b in range(bsz)], axis=0)
    qc = jnp.concatenate([q_ref[:, b * half:(b + 1) * half] for b in range(bsz)], axis=0)
    out_l = (jnp.dot(pc, wp_ref[...], preferred_element_type=F32)
             + jnp.dot(qc, wq_ref[...], preferred_element_type=F32))
    lane = lax.broadcasted_iota(jnp.int32, (tt, LANES), 1)
    nyq_blk = p_ref[:, bsz * half:].astype(F32)
    nyq = jnp.concatenate([jnp.sum(jnp.where(lane == b, nyq_blk, 0.0), axis=1, keepdims=True)
                           for b in range(bsz)], axis=0)
    out_l = out_l + nyq * wn_ref[0:1, :]
    _residual_mlp(out_l, h_ref, mod_ref, g_ref[...], w1_ref, w2_ref, o_ref)


def _post_even(y_pairs, na, h, mod, gains, w_glu, w_out, w1, w2, layer, e):
    bsz, seq, d = h.shape
    npair = y_pairs.shape[0]
    pw = S5_T * S5_PAIR
    prow = TOK_TILE // S5_T * bsz
    return pl.pallas_call(
        _post_even_kernel,
        out_shape=jax.ShapeDtypeStruct(h.shape, F32),
        grid=_token_grid(seq, TOK_TILE),
        in_specs=[pl.BlockSpec((npair, prow, pw), lambda i: (0, i, 0)), _tok_spec(bsz, na.shape[-1], TOK_TILE),
                  _tok_spec(bsz, d, TOK_TILE), _layer_spec(mod, layer), _layer_spec(gains, layer),
                  _layer_spec(w_glu, e), _layer_spec(w_out, e), _layer_spec(w1, layer), _layer_spec(w2, layer)],
        out_specs=_tok_spec(bsz, d, TOK_TILE),
        scratch_shapes=[pltpu.VMEM((npair * S5_PAIR // LANES, bsz * TOK_TILE, LANES), F32)],
        compiler_params=_params(),
        name="post_even",
    )(y_pairs, na, h, mod, gains, w_glu, w_out, w1, w2)


def _post_odd(p, q, h, mod, gains, w_p, w_q, w_n, w1, w2, layer, o):
    bsz, seq, d = h.shape
    tile = TOK_TILE
    rows = lambda arr: pl.BlockSpec((tile, arr.shape[1]), lambda i: (i, 0))
    return pl.pallas_call(
        _post_odd_kernel,
        out_shape=jax.ShapeDtypeStruct(h.shape, F32),
        grid=_token_grid(seq, tile),
        in_specs=[rows(p), rows(q), _tok_spec(bsz, d, tile),
                  _layer_spec(mod, layer), _layer_spec(gains, layer),
                  _layer_spec(w_p, o), _layer_spec(w_q, o), _layer_spec(w_n, o),
                  _layer_spec(w1, layer), _layer_spec(w2, layer)],
        out_specs=_tok_spec(bsz, d, tile),
        compiler_params=_params(),
        name="post_odd",
    )(p, q, h, mod, gains, w_p, w_q, w_n, w1, w2)


def _fnet_seq_dft(dft, a, b, tm, tk):
    seq = dft.shape[0]
    p = _matmul(dft, a, tm, a.shape[1] // 3, tk, BF16)
    q = _matmul(dft, b, tm, b.shape[1] // 2, tk, BF16, a_col0=seq // tk)
    return p, q


def kernel(x, c, ctx, c_ctx, w_mod, b_mod, norm_g, w_in, w_out_even, s5_lam_re, s5_lam_im, s5_log_dt,
           s5_b_re, s5_b_im, s5_c_re, s5_c_im, s5_d, s5_w_glu, na_rpb, w_fourier, w_ff1, w_ff2):
    bsz, seq, d = x.shape
    lc = ctx.shape[1]
    depth = w_mod.shape[0]
    s5w = s5_d.shape[-1]
    last_ctx_layer = 2 * ((depth - 1) // 2)

    mod_rows = 16
    a = jnp.concatenate([c, c_ctx[None, :], jnp.zeros((mod_rows - bsz - 1, d), F32)], axis=0)
    mod = _modulation(a, w_mod, b_mod)
    pad_mod = lambda m: jnp.pad(m, ((0, 0), (0, 0), (0, 8 - N_MOD), (0, 0)))
    mod_l = pad_mod(mod[:, :bsz].reshape(depth, bsz, N_MOD, d))
    mod_c = pad_mod(jnp.broadcast_to(mod[:, bsz:bsz + 1].reshape(depth, 1, N_MOD, d), (depth, bsz, N_MOD, d)))
    gains = jnp.pad(norm_g.astype(F32), ((0, 0), (0, 4), (0, 0)))

    na_w = (w_in.shape[-1] - s5w) // 3
    qscale = jnp.concatenate([jnp.ones((s5w,), F32), jnp.full((na_w,), NA_HEAD_DIM ** -0.5, F32),
                              jnp.ones((2 * na_w,), F32)])
    w_in_b = (w_in * qscale).astype(BF16)
    w_out_b = w_out_even.astype(BF16)
    w_glu_b = s5_w_glu.astype(BF16)
    half = d // 2
    wf_lo = w_fourier[:, 1:half]
    wf_hi = w_fourier[:, half + 1:][:, ::-1]
    w_p_b = jnp.concatenate([w_fourier[:, 0:1], wf_lo + wf_hi], axis=1).astype(BF16)
    w_q_b = jnp.concatenate([jnp.zeros_like(w_fourier[:, 0:1]), wf_lo - wf_hi], axis=1).astype(BF16)
    w_n = jnp.pad(w_fourier[:, half:half + 1].astype(F32), ((0, 0), (0, 7), (0, 0)))
    w1_b = w_ff1.astype(BF16)
    w2_b = w_ff2.astype(BF16)

    wbig, wout, dec, dsk = _s5_prepare(s5_lam_re, s5_lam_im, s5_log_dt, s5_b_re, s5_b_im,
                                       s5_c_re, s5_c_im, s5_d)
    dft_d = _dft_pair(d, right=True)
    dft_dh = jnp.concatenate([dft_d[:, :half], dft_d[:, d:d + half],
                              jnp.broadcast_to(dft_d[:, half:half + 1], (d, LANES))], axis=1)
    dft_l = _dft_pair(seq, right=False)
    dft_c = _dft_pair(lc, right=False)

    bias = _na_bias_table(na_rpb.reshape((-1,) + na_rpb.shape[2:]), seq // GRID_W)

    h, s = x, ctx
    for layer in range(depth):
        upd_ctx = layer < last_ctx_layer
        if layer % 2 == 0:
            e = layer // 2
            u_l, qkv_l = _pre_even(h, mod_l, gains, w_in_b, layer, e)
            u_c, qkv_c = _pre_even(s, mod_c, gains, w_in_b, layer, e)
            y_c, y_l = _s5_chunked(u_c, u_l, wbig, wout, dec, dsk, e, bsz)
            na_l = _na_latent(qkv_l, qkv_c, bias, e)
            h_new = _post_even(y_l, na_l, h, mod_l, gains, w_glu_b, w_out_b, w1_b, w2_b, layer, e)
            if upd_ctx:
                na_c = _na_context(qkv_c)
                s = _post_even(y_c, na_c, s, mod_c, gains, w_glu_b, w_out_b, w1_b, w2_b, layer, e)
            h = h_new
        else:
            o = layer // 2
            a_l, b_l = _pre_odd(h, mod_l, gains, dft_dh, layer)
            p_l, q_l = _fnet_seq_dft(dft_l, a_l, b_l, 1024, 1024)
            h_new = _post_odd(p_l, q_l, h, mod_l, gains, w_p_b, w_q_b, w_n, w1_b, w2_b, layer, o)
            if upd_ctx:
                a_c, b_c = _pre_odd(s, mod_c, gains, dft_dh, layer)
                p_c, q_c = _fnet_seq_dft(dft_c, a_c, b_c, lc, lc)
                s = _post_odd(p_c, q_c, s, mod_c, gains, w_p_b, w_q_b, w_n, w1_b, w2_b, layer, o)
            h = h_new
    return h
```

```python
import functools
import math

import jax
import jax.numpy as jnp
import numpy as np
from jax import lax
from jax.experimental import pallas as pl
from jax.experimental.pallas import tpu as pltpu

F32 = jnp.float32
BF16 = jnp.bfloat16
EPS = 1e-6
NEG_INF = -1e30

GRID_W = 64
S5_GROUP = 16
S5_STATE = 64
S5_T = 16
S5_PAIR = 2 * S5_GROUP
S5_BLOCK = 16
LANES = 128
TOK_TILE = 128
FF_CHUNK = 1024
NA_HEAD_DIM = 64
NA_WIN_R = 8
NA_WIN_C = 16
NA_QROWS = 4
NA_KROWS = 12
N_MOD = 6

V7X_VMEM_LIMIT_BYTES = 56 * 1024 * 1024
HIGHEST = lax.Precision.HIGHEST


def _params(**kw):
    return pltpu.CompilerParams(vmem_limit_bytes=V7X_VMEM_LIMIT_BYTES, **kw)


def _resident(shape):
    nd = len(shape)
    return pl.BlockSpec(shape, lambda *_: (0,) * nd, pipeline_mode=pl.Buffered(1))


def _layer_spec(stacked, idx):
    nd = stacked.ndim
    return pl.BlockSpec((None,) + stacked.shape[1:], lambda *_: (idx,) + (0,) * (nd - 1),
                        pipeline_mode=pl.Buffered(1))


def _rms(x, g):
    ms = jnp.mean(x * x, axis=-1, keepdims=True)
    return x * lax.rsqrt(ms + EPS) * g


def _sigmoid(x):
    return 1.0 / (1.0 + jnp.exp(-x))


def _gelu_tanh(x):
    c = math.sqrt(2.0 / math.pi)
    return x * (0.5 * (1.0 + jnp.tanh(c * (x + 0.044715 * (x * x * x)))))


def _mod_kernel(a_ref, w_ref, b_ref, o_ref):
    a = a_ref[...]
    act = a * _sigmoid(a)
    o_ref[0] = jnp.dot(act.astype(BF16), w_ref[0].astype(BF16), preferred_element_type=F32) + b_ref[0]


def _modulation(a, w_mod, b_mod):
    depth, d, n = w_mod.shape
    rows = a.shape[0]
    tn = 1536
    return pl.pallas_call(
        _mod_kernel,
        out_shape=jax.ShapeDtypeStruct((depth, rows, n), F32),
        grid=(depth, n // tn),
        in_specs=[
            pl.BlockSpec((rows, d), lambda l, j: (0, 0)),
            pl.BlockSpec((1, d, tn), lambda l, j: (l, 0, j)),
            pl.BlockSpec((1, 1, tn), lambda l, j: (l, 0, j)),
        ],
        out_specs=pl.BlockSpec((1, rows, tn), lambda l, j: (l, 0, j)),
        compiler_params=_params(),
        name="adaln_mod",
    )(a, w_mod, b_mod.reshape(depth, 1, n))


def _lane_quarter(shape):
    return lax.broadcasted_iota(jnp.int32, shape, 1) // S5_PAIR


def _tokens_to_pairs(tok_ref, pair_ref, bsz):
    per = LANES // S5_PAIR
    tt = tok_ref.shape[1] // bsz
    n_chunks = tt // S5_T
    quarter = _lane_quarter((n_chunks * bsz, LANES))
    for blk in range(tok_ref.shape[0]):
        for m in range(S5_T // per):
            src = [jnp.concatenate([tok_ref[blk, pl.ds(c * S5_T + per * m + jj, bsz, stride=tt), :]
                                    for c in range(n_chunks)], axis=0) for jj in range(per)]
            for qq in range(per):
                dest = None
                for jj in range(per):
                    shift = ((jj - qq) % per) * S5_PAIR
                    r = src[jj] if shift == 0 else pltpu.roll(src[jj], shift, axis=1)
                    dest = r if dest is None else jnp.where(quarter == jj, r, dest)
                pair_ref[per * blk + qq, :, m * LANES:(m + 1) * LANES] = dest.astype(pair_ref.dtype)


def _pairs_to_tokens(pair_ref, tok_ref, bsz):
    per = LANES // S5_PAIR
    tt = tok_ref.shape[1] // bsz
    n_chunks = tt // S5_T
    quarter = _lane_quarter((n_chunks * bsz, LANES))
    for blk in range(tok_ref.shape[0]):
        for m in range(S5_T // per):
            src = [pair_ref[per * blk + qq, :, m * LANES:(m + 1) * LANES].astype(F32) for qq in range(per)]
            for ts in range(per):
                dest = None
                for qq in range(per):
                    shift = ((qq - ts) % per) * S5_PAIR
                    r = src[qq] if shift == 0 else pltpu.roll(src[qq], shift, axis=1)
                    dest = r if dest is None else jnp.where(quarter == qq, r, dest)
                for c in range(n_chunks):
                    tok_ref[blk, pl.ds(c * S5_T + per * m + ts, bsz, stride=tt), :] = dest[c * bsz:(c + 1) * bsz]


def _modulated(h_ref, mod_ref, gain, shift_row, scale_row):
    parts = []
    for b in range(h_ref.shape[0]):
        m = mod_ref[b]
        y = _rms(h_ref[b], gain) * (1.0 + m[scale_row:scale_row + 1]) + m[shift_row:shift_row + 1]
        parts.append(y.astype(BF16))
    return jnp.concatenate(parts, axis=0)


def _pre_even_kernel(h_ref, mod_ref, g_ref, w_ref, up_ref, qkv_ref, us_ref, *, s5_width):
    bsz, tt, _ = h_ref.shape
    hl = _modulated(h_ref, mod_ref, g_ref[0:1], 0, 1)
    z = jnp.dot(hl, w_ref[...], preferred_element_type=F32)
    for blk in range(s5_width // LANES):
        us_ref[blk] = z[:, blk * LANES:(blk + 1) * LANES]
    for b in range(bsz):
        qkv_ref[b] = z[b * tt:(b + 1) * tt, s5_width:].astype(BF16)
    _tokens_to_pairs(us_ref, up_ref, bsz)


def _token_grid(seq, tile):
    return (seq // tile,)


def _tok_spec(bsz, width, tile):
    return pl.BlockSpec((bsz, tile, width), lambda i: (0, i, 0))


def _pre_even(h, mod, gains, w_in, layer, e):
    bsz, seq, d = h.shape
    n = w_in.shape[-1]
    s5w = d // 2
    npair = s5w // S5_PAIR
    pw = S5_T * S5_PAIR
    tile = TOK_TILE
    prow = tile // S5_T * bsz
    return pl.pallas_call(
        functools.partial(_pre_even_kernel, s5_width=s5w),
        out_shape=(jax.ShapeDtypeStruct((npair, seq // S5_T * bsz, pw), BF16),
                   jax.ShapeDtypeStruct((bsz, seq, n - s5w), BF16)),
        grid=_token_grid(seq, tile),
        in_specs=[_tok_spec(bsz, d, tile), _layer_spec(mod, layer), _layer_spec(gains, layer),
                  _layer_spec(w_in, e)],
        out_specs=(pl.BlockSpec((npair, prow, pw), lambda i: (0, i, 0)), _tok_spec(bsz, n - s5w, tile)),
        scratch_shapes=[pltpu.VMEM((s5w // LANES, bsz * tile, LANES), F32)],
        compiler_params=_params(),
        name="pre_even",
    )(h, mod, gains, w_in)


def _pre_odd_kernel(h_ref, mod_ref, g_ref, w_ref, a_ref, b_ref, *, half):
    bsz, tt, _ = h_ref.shape
    hl = _modulated(h_ref, mod_ref, g_ref[0:1], 0, 1)
    z = jnp.dot(hl, w_ref[...], preferred_element_type=F32)
    lane = lax.broadcasted_iota(jnp.int32, (tt, LANES), 1)
    nyq = jnp.zeros((tt, LANES), F32)
    for b in range(bsz):
        zb = z[b * tt:(b + 1) * tt]
        a_ref[:, b * half:(b + 1) * half] = zb[:, :half].astype(BF16)
        b_ref[:, b * half:(b + 1) * half] = zb[:, half:2 * half].astype(BF16)
        nyq = jnp.where(lane == b, zb[:, 2 * half:], nyq)
    a_ref[:, bsz * half:] = nyq.astype(BF16)


def _pre_odd(h, mod, gains, w_cs, layer):
    bsz, seq, d = h.shape
    half = d // 2
    return pl.pallas_call(
        functools.partial(_pre_odd_kernel, half=half),
        out_shape=(jax.ShapeDtypeStruct((seq, bsz * half + LANES), BF16),
                   jax.ShapeDtypeStruct((seq, bsz * half), BF16)),
        grid=_token_grid(seq, TOK_TILE),
        in_specs=[_tok_spec(bsz, d, TOK_TILE), _layer_spec(mod, layer), _layer_spec(gains, layer),
                  _resident(w_cs.shape)],
        out_specs=(pl.BlockSpec((TOK_TILE, bsz * half + LANES), lambda i: (i, 0)),
                   pl.BlockSpec((TOK_TILE, bsz * half), lambda i: (i, 0))),
        compiler_params=_params(),
        name="pre_odd",
    )(h, mod, gains, w_cs)


def _matmul_kernel(a_ref, b_ref, o_ref, acc_ref, *, nk):
    k = pl.program_id(2)

    @pl.when(k == 0)
    def _():
        acc_ref[...] = jnp.zeros_like(acc_ref)

    acc_ref[...] += jnp.dot(a_ref[...], b_ref[...], preferred_element_type=F32)

    @pl.when(k == nk - 1)
    def _():
        o_ref[...] = acc_ref[...].astype(o_ref.dtype)


def _matmul(a, b, tm, tn, tk, out_dtype, a_col0=0):
    m = a.shape[0]
    kdim, n = b.shape
    nk = kdim // tk
    return pl.pallas_call(
        functools.partial(_matmul_kernel, nk=nk),
        out_shape=jax.ShapeDtypeStruct((m, n), out_dtype),
        grid=(m // tm, n // tn, nk),
        in_specs=[pl.BlockSpec((tm, tk), lambda i, j, k: (i, k + a_col0)),
                  pl.BlockSpec((tk, tn), lambda i, j, k: (k, j))],
        out_specs=pl.BlockSpec((tm, tn), lambda i, j, k: (i, j)),
        scratch_shapes=[pltpu.VMEM((tm, tn), F32)],
        compiler_params=_params(),
        name="fnet_seq_dft",
    )(a, b)


def _dft_pair(n, right):
    n0 = LANES
    n1 = n // n0
    k = jnp.arange(n, dtype=jnp.int32)[:, None]
    pa = (k * (jnp.arange(n1, dtype=jnp.int32)[None, :] * n0)) % n
    pb = (k * jnp.arange(n0, dtype=jnp.int32)[None, :]) % n
    w = 2.0 * math.pi / n
    scale = 1.0 / math.sqrt(n)
    sign = 1.0 if right else -1.0
    ca, sa = jnp.cos(pa.astype(F32) * w) * scale, jnp.sin(pa.astype(F32) * w) * scale
    cb, sb = jnp.cos(pb.astype(F32) * w), jnp.sin(pb.astype(F32) * w)
    a1 = jnp.concatenate([ca, sign * sa], axis=1)
    a2 = jnp.concatenate([sa, -sign * ca], axis=1)
    tr = min(n, 256)
    rows = lambda w_: pl.BlockSpec((tr, w_), lambda i: (i, 0))
    return pl.pallas_call(
        _dft_expand_kernel,
        out_shape=jax.ShapeDtypeStruct((n, 2 * n), BF16),
        grid=(n // tr,),
        in_specs=[rows(2 * n1), rows(2 * n1), rows(n0), rows(n0)],
        out_specs=rows(2 * n),
        compiler_params=_params(),
        name="dft_expand",
    )(a1, a2, cb, sb)


def _dft_expand_kernel(a1_ref, a2_ref, cb_ref, sb_ref, o_ref):
    cb, sb = cb_ref[...], sb_ref[...]
    for j in range(a1_ref.shape[1]):
        blk = a1_ref[:, j:j + 1] * cb - a2_ref[:, j:j + 1] * sb
        o_ref[:, j * LANES:(j + 1) * LANES] = blk.astype(BF16)


def _s5_prepare(lam_re, lam_im, log_dt, b_re, b_im, c_re, c_im, d_skip):
    t_len, hh, pp = S5_T, S5_GROUP, S5_STATE
    ne, _, gg, _ = lam_re.shape
    lam_re = jnp.minimum(lam_re.astype(F32), -1e-4)
    lam_im = lam_im.astype(F32)
    dt = jnp.exp(log_dt.astype(F32))[..., None]
    mag = jnp.exp(lam_re * dt)
    a_re = mag * jnp.cos(lam_im * dt)
    a_im = mag * jnp.sin(lam_im * dt)
    den = lam_re * lam_re + lam_im * lam_im
    num_re = a_re - 1.0
    f_re = (num_re * lam_re + a_im * lam_im) / den
    f_im = (a_im * lam_re - num_re * lam_im) / den
    b_re = b_re.astype(F32)
    b_im = b_im.astype(F32)
    bb_re = f_re[..., None] * b_re - f_im[..., None] * b_im
    bb_im = f_re[..., None] * b_im + f_im[..., None] * b_re
    tau = jnp.arange(t_len + 1, dtype=F32).reshape(1, 1, 1, -1, 1)
    lr, li, dtt = lam_re[..., None, :], lam_im[..., None, :], dt[..., None]
    pmag = jnp.exp(lr * dtt * tau)
    p_re = pmag * jnp.cos(li * dtt * tau)
    p_im = pmag * jnp.sin(li * dtt * tau)
    bbt_re = jnp.swapaxes(bb_re, -1, -2)[:, :, :, None]
    bbt_im = jnp.swapaxes(bb_im, -1, -2)[:, :, :, None]
    abt_re = p_re[..., None, :] * bbt_re - p_im[..., None, :] * bbt_im
    abt_im = p_re[..., None, :] * bbt_im + p_im[..., None, :] * bbt_re
    c_re = c_re.astype(F32)[:, :, :, None]
    c_im = c_im.astype(F32)[:, :, :, None]
    e_re = c_re * p_re[..., None, :] - c_im * p_im[..., None, :]
    e_im = c_re * p_im[..., None, :] + c_im * p_re[..., None, :]

    gp = gg // 2
    pw = t_len * 2 * hh

    def cat(re, im):
        tn = re.shape[2]
        re = re.reshape(ne, gp, 2, tn, hh, pp)
        im = im.reshape(ne, gp, 2, tn, hh, pp)
        z = jnp.zeros((ne, gp, tn, hh, pp), F32)
        e0 = jnp.concatenate([re[:, :, 0], z, im[:, :, 0], z], axis=-1)
        e1 = jnp.concatenate([z, re[:, :, 1], z, im[:, :, 1]], axis=-1)
        return jnp.stack([e0, e1], axis=3).reshape(ne, gp, tn * 2 * hh, 4 * pp)

    fwd_t = slice(0, t_len)
    rev_t = slice(t_len - 1, None, -1)
    st_cols, out_rows, resp = [], [], []
    for d in range(2):
        tau_k = fwd_t if d == 0 else rev_t
        bcat0 = cat(abt_re[:, d, :, 0:1], abt_im[:, d, :, 0:1])
        ecat = cat(e_re[:, d, :, tau_k], -e_im[:, d, :, tau_k])
        resp.append(jnp.einsum('xqrk,xqck->xqrc', bcat0, ecat, precision=HIGHEST))
        tau_s = rev_t if d == 0 else fwd_t
        st_cols.append(cat(abt_re[:, d, :, tau_s], abt_im[:, d, :, tau_s]))
        tau_o = slice(1, t_len + 1) if d == 0 else slice(t_len, 0, -1)
        out_rows.append(jnp.swapaxes(cat(e_re[:, d, :, tau_o], -e_im[:, d, :, tau_o]), -1, -2))

    span = pw - 2 * hh
    kf = jnp.pad(resp[0], ((0, 0), (0, 0), (0, 0), (span, 0)))
    kb = jnp.pad(resp[1], ((0, 0), (0, 0), (0, 0), (0, span)))
    blocks = []
    for j in range(t_len):
        lo_f = span - 2 * hh * j
        lo_b = 2 * hh * (t_len - 1 - j)
        blocks.append(kf[..., lo_f:lo_f + pw] + kb[..., lo_b:lo_b + pw])
    toep = jnp.stack(blocks, axis=2).reshape(ne, gp, pw, pw)
    wbig = jnp.concatenate([toep] + st_cols, axis=-1).astype(BF16)
    wout = jnp.concatenate(out_rows, axis=2).astype(BF16)

    dec_rows = [p_re[:, 0, :, t_len], p_im[:, 0, :, t_len], p_re[:, 1, :, t_len], p_im[:, 1, :, t_len]]
    dec = jnp.stack([x.reshape(ne, gp, 2 * pp) for x in dec_rows], axis=2)
    dec = jnp.pad(dec, ((0, 0), (0, 0), (0, 4), (0, 0)))
    dsk = jnp.broadcast_to(d_skip.astype(F32).reshape(ne, gp, 1, 2 * hh), (ne, gp, t_len, 2 * hh))
    dsk = jnp.pad(dsk.reshape(ne, gp, 1, pw), ((0, 0), (0, 0), (0, 7), (0, 0)))
    return wbig, wout, dec, dsk


def _s5_kernel(uc_ref, ul_ref, wbig_ref, wout_ref, dec_ref, dsk_ref, yc_ref, yl_ref, z_ref, sin_ref, car_ref,
               *, bsz):
    tr = z_ref.shape[1]
    nb_c, nb_l = uc_ref.shape[1] // tr, ul_ref.shape[1] // tr
    nb = nb_c + nb_l
    wy = S5_T * S5_PAIR
    sw = 2 * S5_STATE
    parts = ((uc_ref, yc_ref, 0, nb_c), (ul_ref, yl_ref, nb_c, nb_l))

    for u_ref, _, base, nblk in parts:
        def stage1(i, carry, u_ref=u_ref, base=base):
            r = pl.multiple_of(i * tr, tr)
            ub = u_ref[0, pl.ds(r, tr), :].astype(BF16)
            z_ref[base + i] = jnp.dot(ub, wbig_ref[0, :, wy:], preferred_element_type=F32)
            return carry

        lax.fori_loop(0, nblk, stage1, 0)

    dec = dec_ref[0]
    entry_gain = []
    for d in range(2):
        a_re = dec[2 * d:2 * d + 1][None]
        a_im = dec[2 * d + 1:2 * d + 2][None]
        lc = 2 * d * sw
        sc = 2 * d * sw
        powers = [(jnp.ones_like(a_re), jnp.zeros_like(a_re))]
        for _ in range(S5_BLOCK):
            p_re, p_im = powers[-1]
            powers.append((p_re * a_re - p_im * a_im, p_re * a_im + p_im * a_re))
        steps = list(range(S5_BLOCK)) if d == 0 else list(range(S5_BLOCK - 1, -1, -1))
        if d == 0:
            block_order = list(range(nb))
        else:
            block_order = list(range(nb_c - 1, -1, -1)) + list(range(nb - 1, nb_c - 1, -1))

        s_re = jnp.zeros((nb, bsz, sw), F32)
        s_im = jnp.zeros((nb, bsz, sw), F32)
        for i in steps:
            rs = slice(i * bsz, (i + 1) * bsz)
            sin_ref[:, rs, sc:sc + sw] = s_re
            sin_ref[:, rs, sc + sw:sc + 2 * sw] = s_im
            l_re = z_ref[:, rs, lc:lc + sw]
            l_im = z_ref[:, rs, lc + sw:lc + 2 * sw]
            s_re, s_im = a_re * s_re - a_im * s_im + l_re, a_re * s_im + a_im * s_re + l_im

        g_re, g_im = powers[S5_BLOCK]
        c_re = jnp.zeros((bsz, sw), F32)
        c_im = jnp.zeros((bsz, sw), F32)
        for blk in block_order:
            car_ref[blk, :, sc:sc + sw] = c_re
            car_ref[blk, :, sc + sw:sc + 2 * sw] = c_im
            c_re, c_im = (g_re[0] * c_re - g_im[0] * c_im + s_re[blk], g_re[0] * c_im + g_im[0] * c_re + s_im[blk])

        since = {i: n for n, i in enumerate(steps)}
        entry_gain.append(tuple(
            jnp.concatenate([jnp.broadcast_to(powers[since[i]][part][0], (bsz, sw)) for i in range(S5_BLOCK)], axis=0)
            for part in range(2)))

    dsk = dsk_ref[0][0:1]

    for u_ref, y_ref, base, nblk in parts:
        def stage3(i, carry, u_ref=u_ref, y_ref=y_ref, base=base):
            r = pl.multiple_of(i * tr, tr)
            u = u_ref[0, pl.ds(r, tr), :]
            states = []
            for d in range(2):
                sc = 2 * d * sw
                p_re, p_im = entry_gain[d]
                e_re = jnp.concatenate([car_ref[base + i, :, sc:sc + sw]] * S5_BLOCK, axis=0)
                e_im = jnp.concatenate([car_ref[base + i, :, sc + sw:sc + 2 * sw]] * S5_BLOCK, axis=0)
                states.append(sin_ref[base + i, :, sc:sc + sw] + (p_re * e_re - p_im * e_im))
                states.append(sin_ref[base + i, :, sc + sw:sc + 2 * sw] + (p_re * e_im + p_im * e_re))
            sb = jnp.concatenate(states, axis=1).astype(BF16)
            y = (dsk * u.astype(F32) + jnp.dot(u.astype(BF16), wbig_ref[0, :, 0:wy], preferred_element_type=F32)
                 + jnp.dot(sb, wout_ref[0], preferred_element_type=F32))
            y_ref[0, pl.ds(r, tr), :] = y.astype(y_ref.dtype)
            return carry

        lax.fori_loop(0, nblk, stage3, 0)


def _s5_chunked(u_ctx, u_lat, wbig, wout, dec, dsk, e, bsz):
    gp, rows_c, width = u_ctx.shape
    rows_l = u_lat.shape[1]
    zc = wbig.shape[-1]
    tr = S5_BLOCK * bsz
    nb = (rows_c + rows_l) // tr
    per_pair = lambda shape: pl.BlockSpec((1,) + shape[1:], lambda q: (q, 0, 0))
    of_layer = lambda w: pl.BlockSpec((None, 1) + w.shape[2:], lambda q: (e, q, 0, 0))
    return pl.pallas_call(
        functools.partial(_s5_kernel, bsz=bsz),
        out_shape=(jax.ShapeDtypeStruct(u_ctx.shape, BF16), jax.ShapeDtypeStruct(u_lat.shape, BF16)),
        grid=(gp,),
        in_specs=[per_pair(u_ctx.shape), per_pair(u_lat.shape), of_layer(wbig), of_layer(wout),
                  of_layer(dec), of_layer(dsk)],
        out_specs=(per_pair(u_ctx.shape), per_pair(u_lat.shape)),
        scratch_shapes=[pltpu.VMEM((nb, tr, zc - width), F32), pltpu.VMEM((nb, tr, width), F32),
                        pltpu.VMEM((nb, bsz, 8 * S5_STATE), F32)],
        compiler_params=_params(),
        name="s5_chunked",
    )(u_ctx, u_lat, wbig, wout, dec, dsk)


def _attend_pairs(q_ref, k_refs, v_refs, bias_ref, o_ref, n_biased):
    width = q_ref.shape[-1]
    kb = k_refs[0].shape[1]
    lane = lax.broadcasted_iota(jnp.int32, (1, 2 * NA_HEAD_DIM), 1)
    outs = []
    for hp in range(width // (2 * NA_HEAD_DIM)):
        cs = slice(hp * 2 * NA_HEAD_DIM, (hp + 1) * 2 * NA_HEAD_DIM)
        qp = q_ref[0, :, cs]
        ks = [r[0, :, cs] for r in k_refs]
        vs = [r[0, :, cs] for r in v_refs]
        res = []
        for e in range(2):
            sel = (lane // NA_HEAD_DIM) == e
            qe = jnp.where(sel, qp, jnp.zeros_like(qp))
            parts = []
            for i, kk in enumerate(ks):
                s = lax.dot_general(qe, kk, (((1,), (1,)), ((), ())), preferred_element_type=F32)
                if i < n_biased:
                    s = s + bias_ref[0, 2 * hp + e, :, i * kb:(i + 1) * kb]
                parts.append(s)
            m = parts[0].max(axis=-1, keepdims=True)
            for s in parts[1:]:
                m = jnp.maximum(m, s.max(axis=-1, keepdims=True))
            ps = [jnp.exp(s - m) for s in parts]
            den = ps[0].sum(axis=-1, keepdims=True)
            for p in ps[1:]:
                den = den + p.sum(axis=-1, keepdims=True)
            acc = jnp.dot(ps[0].astype(BF16), vs[0], preferred_element_type=F32)
            for p, vv in zip(ps[1:], vs[1:]):
                acc = acc + jnp.dot(p.astype(BF16), vv, preferred_element_type=F32)
            res.append(acc / den)
        outs.append(jnp.where(lane < NA_HEAD_DIM, res[0], res[1]))
    o_ref[0] = jnp.concatenate(outs, axis=-1).astype(o_ref.dtype)


def _na_lat_kernel(q_ref, k0, k1, k2, v0, v1, v2, kc, vc, bias_ref, o_ref):
    _attend_pairs(q_ref, (k0, k1, k2, kc), (v0, v1, v2, vc), bias_ref, o_ref, 3)


def _na_ctx_kernel(q_ref, kc, vc, o_ref):
    _attend_pairs(q_ref, (kc,), (vc,), None, o_ref, 0)


def _na_bias_table(rpb, rows):
    nh = rpb.shape[0]
    nblk = rows // NA_QROWS
    blocks = np.array([0, 1, nblk - 1])
    kb0 = np.clip(blocks - 1, 0, nblk - NA_KROWS // NA_QROWS) * NA_QROWS
    qr = blocks[:, None] * NA_QROWS + np.arange(NA_QROWS)[None, :]
    r0 = np.clip(qr - NA_WIN_R // 2, 0, rows - NA_WIN_R)
    kr = kb0[:, None] + np.arange(NA_KROWS)[None, :]
    row_ok = (kr[:, None, :] >= r0[:, :, None]) & (kr[:, None, :] < r0[:, :, None] + NA_WIN_R)
    n_dr = 2 * NA_WIN_R - 1
    dr = kr[:, None, :] - qr[:, :, None] + (NA_WIN_R - 1)
    cols = np.arange(GRID_W)
    c0 = np.clip(cols - NA_WIN_C // 2, 0, GRID_W - NA_WIN_C)
    col_ok = (cols[None, :] >= c0[:, None]) & (cols[None, :] < c0[:, None] + NA_WIN_C)
    dc = np.clip(cols[None, :] - cols[:, None], -(NA_WIN_C - 1), NA_WIN_C - 1) + (NA_WIN_C - 1)
    sel_c = (np.arange(2 * NA_WIN_C - 1)[:, None, None] == dc[None]).astype(np.float32)
    by_col = jnp.einsum('hrc,cqk->hrqk', rpb.astype(F32), sel_c, precision=HIGHEST)
    zero = jnp.zeros((nh, 1, GRID_W, GRID_W), F32)
    padded = jnp.concatenate([zero, by_col, zero], axis=1)
    two_rows = jnp.concatenate([padded[:, :-1], padded[:, 1:]], axis=-1)
    variants = []
    for v in range(len(blocks)):
        strips = []
        for i in range(NA_QROWS):
            first = np.clip(dr[v, i, 0::2], -1, n_dr - 1) + 1
            strips.append(jnp.concatenate([two_rows[:, int(s)] for s in first], axis=-1))
        variants.append(jnp.concatenate(strips, axis=1))
    bias = jnp.stack(variants, axis=0)
    ok = (row_ok[:, :, None, :, None] & col_ok[None, None, :, None, :]).reshape(
        len(blocks), 1, NA_QROWS * GRID_W, NA_KROWS * GRID_W)
    return jnp.where(ok, bias, NEG_INF)


def _na_latent(qkv, qkv_c, bias, e):
    bsz, seq, w3 = qkv.shape
    width = w3 // 3
    nh = width // NA_HEAD_DIM
    lc = qkv_c.shape[1]
    qb = NA_QROWS * GRID_W
    nblk = seq // qb
    nkb = NA_KROWS // NA_QROWS

    def kv_spec(col, s):
        return pl.BlockSpec((1, qb, width),
                            lambda b, a: (b, jnp.clip(a - 1, 0, nblk - nkb) + s, col))

    def variant(a):
        return (a > 0).astype(jnp.int32) + (a == nblk - 1).astype(jnp.int32)

    return pl.pallas_call(
        _na_lat_kernel,
        out_shape=jax.ShapeDtypeStruct((bsz, seq, width), BF16),
        grid=(bsz, nblk),
        in_specs=[pl.BlockSpec((1, qb, width), lambda b, a: (b, a, 0))]
        + [kv_spec(1, s) for s in range(nkb)] + [kv_spec(2, s) for s in range(nkb)]
        + [pl.BlockSpec((1, lc, width), lambda b, a: (b, 0, 1)),
           pl.BlockSpec((1, lc, width), lambda b, a: (b, 0, 2)),
           pl.BlockSpec((1, nh) + bias.shape[2:], lambda b, a: (variant(a), e, 0, 0))],
        out_specs=pl.BlockSpec((1, qb, width), lambda b, a: (b, a, 0)),
        compiler_params=_params(),
        name="na_latent",
    )(qkv, *([qkv] * (2 * nkb)), qkv_c, qkv_c, bias)


def _na_context(qkv_c):
    bsz, lc, w3 = qkv_c.shape
    width = w3 // 3
    return pl.pallas_call(
        _na_ctx_kernel,
        out_shape=jax.ShapeDtypeStruct((bsz, lc, width), BF16),
        grid=(bsz,),
        in_specs=[pl.BlockSpec((1, lc, width), lambda b: (b, 0, 0)),
                  pl.BlockSpec((1, lc, width), lambda b: (b, 0, 1)),
                  pl.BlockSpec((1, lc, width), lambda b: (b, 0, 2))],
        out_specs=pl.BlockSpec((1, lc, width), lambda b: (b, 0, 0)),
        compiler_params=_params(),
        name="na_context",
    )(qkv_c, qkv_c, qkv_c)


def _residual_mlp(out_l, h_ref, mod_ref, gains, w1_ref, w2_ref, o_ref):
    bsz, tt, _ = h_ref.shape
    parts = []
    for b in range(bsz):
        m = mod_ref[b]
        h1 = h_ref[b] + m[2:3] * _rms(out_l[b * tt:(b + 1) * tt], gains[1:2])
        o_ref[b] = h1
        parts.append((_rms(h1, gains[2:3]) * (1.0 + m[4:5]) + m[3:4]).astype(BF16))
    hf = jnp.concatenate(parts, axis=0)
    d_ff = w1_ref.shape[1]
    acc = None
    for kf in range(d_ff // FF_CHUNK):
        cs = slice(kf * FF_CHUNK, (kf + 1) * FF_CHUNK)
        hid = jnp.maximum(jnp.dot(hf, w1_ref[:, cs], preferred_element_type=F32), 0.0)
        part = jnp.dot((hid * hid).astype(BF16), w2_ref[cs, :], preferred_element_type=F32)
        acc = part if acc is None else acc + part
    for b in range(bsz):
        o_ref[b] = o_ref[b] + mod_ref[b][5:6] * _rms(acc[b * tt:(b + 1) * tt], gains[3:4])


def _post_even_kernel(yp_ref, na_ref, h_ref, mod_ref, g_ref, wglu_ref, wo_ref, w1_ref, w2_ref, o_ref, ys_ref):
    bsz, tt, na_w = na_ref.shape
    _pairs_to_tokens(yp_ref, ys_ref, bsz)
    g = _gelu_tanh(jnp.concatenate([ys_ref[blk] for blk in range(ys_ref.shape[0])], axis=-1))
    gate = _sigmoid(jnp.dot(g.astype(BF16), wglu_ref[...], preferred_element_type=F32))
    s5 = (g * gate).astype(BF16)
    sw = s5.shape[1]
    out_l = (jnp.dot(s5, wo_ref[0:sw, :], preferred_element_type=F32)
             + jnp.dot(na_ref[...].reshape(bsz * tt, na_w), wo_ref[sw:, :], preferred_element_type=F32))
    _residual_mlp(out_l, h_ref, mod_ref, g_ref[...], w1_ref, w2_ref, o_ref)


def _post_odd_kernel(p_ref, q_ref, h_ref, mod_ref, g_ref, wp_ref, wq_ref, wn_ref, w1_ref, w2_ref, o_ref):
    bsz, tt, _ = h_ref.shape
    half = wp_ref.shape[0]
    pc = jnp.concatenate([p_ref[:, b * half:(b + 1) * half] for b in range(bsz)], axis=0)
    qc = jnp.concatenate([q_ref[:, b * half:(b + 1) * half] for b in range(bsz)], axis=0)
    out_l = (jnp.dot(pc, wp_ref[...], preferred_element_type=F32)
             + jnp.dot(qc, wq_ref[...], preferred_element_type=F32))
    lane = lax.broadcasted_iota(jnp.int32, (tt, LANES), 1)
    nyq_blk = p_ref[:, bsz * half:].astype(F32)
    nyq = jnp.concatenate([jnp.sum(jnp.where(lane == b, nyq_blk, 0.0), axis=1, keepdims=True)
                           for b in range(bsz)], axis=0)
    out_l = out_l + nyq * wn_ref[0:1, :]
    _residual_mlp(out_l, h_ref, mod_ref, g_ref[...], w1_ref, w2_ref, o_ref)


def _post_even(y_pairs, na, h, mod, gains, w_glu, w_out, w1, w2, layer, e):
    bsz, seq, d = h.shape
    npair = y_pairs.shape[0]
    pw = S5_T * S5_PAIR
    prow = TOK_TILE // S5_T * bsz
    return pl.pallas_call(
        _post_even_kernel,
        out_shape=jax.ShapeDtypeStruct(h.shape, F32),
        grid=_token_grid(seq, TOK_TILE),
        in_specs=[pl.BlockSpec((npair, prow, pw), lambda i: (0, i, 0)), _tok_spec(bsz, na.shape[-1], TOK_TILE),
                  _tok_spec(bsz, d, TOK_TILE), _layer_spec(mod, layer), _layer_spec(gains, layer),
                  _layer_spec(w_glu, e), _layer_spec(w_out, e), _layer_spec(w1, layer), _layer_spec(w2, layer)],
        out_specs=_tok_spec(bsz, d, TOK_TILE),
        scratch_shapes=[pltpu.VMEM((npair * S5_PAIR // LANES, bsz * TOK_TILE, LANES), F32)],
        compiler_params=_params(),
        name="post_even",
    )(y_pairs, na, h, mod, gains, w_glu, w_out, w1, w2)


def _post_odd(p, q, h, mod, gains, w_p, w_q, w_n, w1, w2, layer, o):
    bsz, seq, d = h.shape
    tile = TOK_TILE
    rows = lambda arr: pl.BlockSpec((tile, arr.shape[1]), lambda i: (i, 0))
    return pl.pallas_call(
        _post_odd_kernel,
        out_shape=jax.ShapeDtypeStruct(h.shape, F32),
        grid=_token_grid(seq, tile),
        in_specs=[rows(p), rows(q), _tok_spec(bsz, d, tile),
                  _layer_spec(mod, layer), _layer_spec(gains, layer),
                  _layer_spec(w_p, o), _layer_spec(w_q, o), _layer_spec(w_n, o),
                  _layer_spec(w1, layer), _layer_spec(w2, layer)],
        out_specs=_tok_spec(bsz, d, tile),
        compiler_params=_params(),
        name="post_odd",
    )(p, q, h, mod, gains, w_p, w_q, w_n, w1, w2)


def _fnet_seq_dft(dft, a, b, tm, tk):
    seq = dft.shape[0]
    p = _matmul(dft, a, tm, a.shape[1] // 3, tk, BF16)
    q = _matmul(dft, b, tm, b.shape[1] // 2, tk, BF16, a_col0=seq // tk)
    return p, q


def kernel(x, c, ctx, c_ctx, w_mod, b_mod, norm_g, w_in, w_out_even, s5_lam_re, s5_lam_im, s5_log_dt,
           s5_b_re, s5_b_im, s5_c_re, s5_c_im, s5_d, s5_w_glu, na_rpb, w_fourier, w_ff1, w_ff2):
    bsz, seq, d = x.shape
    lc = ctx.shape[1]
    depth = w_mod.shape[0]
    s5w = s5_d.shape[-1]
    last_ctx_layer = 2 * ((depth - 1) // 2)

    mod_rows = 16
    a = jnp.concatenate([c, c_ctx[None, :], jnp.zeros((mod_rows - bsz - 1, d), F32)], axis=0)
    mod = _modulation(a, w_mod, b_mod)
    pad_mod = lambda m: jnp.pad(m, ((0, 0), (0, 0), (0, 8 - N_MOD), (0, 0)))
    mod_l = pad_mod(mod[:, :bsz].reshape(depth, bsz, N_MOD, d))
    mod_c = pad_mod(jnp.broadcast_to(mod[:, bsz:bsz + 1].reshape(depth, 1, N_MOD, d), (depth, bsz, N_MOD, d)))
    gains = jnp.pad(norm_g.astype(F32), ((0, 0), (0, 4), (0, 0)))

    na_w = (w_in.shape[-1] - s5w) // 3
    qscale = jnp.concatenate([jnp.ones((s5w,), F32), jnp.full((na_w,), NA_HEAD_DIM ** -0.5, F32),
                              jnp.ones((2 * na_w,), F32)])
    w_in_b = (w_in * qscale).astype(BF16)
    w_out_b = w_out_even.astype(BF16)
    w_glu_b = s5_w_glu.astype(BF16)
    half = d // 2
    wf_lo = w_fourier[:, 1:half]
    wf_hi = w_fourier[:, half + 1:][:, ::-1]
    w_p_b = jnp.concatenate([w_fourier[:, 0:1], wf_lo + wf_hi], axis=1).astype(BF16)
    w_q_b = jnp.concatenate([jnp.zeros_like(w_fourier[:, 0:1]), wf_lo - wf_hi], axis=1).astype(BF16)
    w_n = jnp.pad(w_fourier[:, half:half + 1].astype(F32), ((0, 0), (0, 7), (0, 0)))
    w1_b = w_ff1.astype(BF16)
    w2_b = w_ff2.astype(BF16)

    wbig, wout, dec, dsk = _s5_prepare(s5_lam_re, s5_lam_im, s5_log_dt, s5_b_re, s5_b_im,
                                       s5_c_re, s5_c_im, s5_d)
    dft_d = _dft_pair(d, right=True)
    dft_dh = jnp.concatenate([dft_d[:, :half], dft_d[:, d:d + half],
                              jnp.broadcast_to(dft_d[:, half:half + 1], (d, LANES))], axis=1)
    dft_l = _dft_pair(seq, right=False)
    dft_c = _dft_pair(lc, right=False)

    bias = _na_bias_table(na_rpb.reshape((-1,) + na_rpb.shape[2:]), seq // GRID_W)

    h, s = x, ctx
    for layer in range(depth):
        upd_ctx = layer < last_ctx_layer
        if layer % 2 == 0:
            e = layer // 2
            u_l, qkv_l = _pre_even(h, mod_l, gains, w_in_b, layer, e)
            u_c, qkv_c = _pre_even(s, mod_c, gains, w_in_b, layer, e)
            y_c, y_l = _s5_chunked(u_c, u_l, wbig, wout, dec, dsk, e, bsz)
            na_l = _na_latent(qkv_l, qkv_c, bias, e)
            h_new = _post_even(y_l, na_l, h, mod_l, gains, w_glu_b, w_out_b, w1_b, w2_b, layer, e)
            if upd_ctx:
                na_c = _na_context(qkv_c)
                s = _post_even(y_c, na_c, s, mod_c, gains, w_glu_b, w_out_b, w1_b, w2_b, layer, e)
            h = h_new
        else:
            o = layer // 2
            a_l, b_l = _pre_odd(h, mod_l, gains, dft_dh, layer)
            p_l, q_l = _fnet_seq_dft(dft_l, a_l, b_l, 1024, 1024)
            h_new = _post_odd(p_l, q_l, h, mod_l, gains, w_p_b, w_q_b, w_n, w1_b, w2_b, layer, o)
            if upd_ctx:
                a_c, b_c = _pre_odd(s, mod_c, gains, dft_dh, layer)
                p_c, q_c = _fnet_seq_dft(dft_c, a_c, b_c, lc, lc)
                s = _post_odd(p_c, q_c, s, mod_c, gains, w_p_b, w_q_b, w_n, w1_b, w2_b, layer, o)
            h = h_new
    return h
```
